```python
import math
import jax, jax.numpy as jnp
from jax import lax
import numpy as np

D_MODEL = 1024
BATCH = 4
SEQ = 8192
DEPTH = 2
DEC_BATCH = 8
DEC_SEQ = 64
PAST_LEN = 1024

CHUNK = 64
N_A = DEPTH // 2
N_B = DEPTH - N_A
RMS_EPS = 1e-5
RW_HEAD = 64
RW_HEADS = D_MODEL // RW_HEAD
D_DECAY_LORA = 64
D_AAA_LORA = 64
D_GATE_LORA = 160
LNX_EPS = 64e-5
HEAD_DIM = 64
N_Q_HEADS = D_MODEL // HEAD_DIM
N_KV_HEADS = 4
GROUP = N_Q_HEADS // N_KV_HEADS
WINDOW = 128
WIN_CHUNKS = WINDOW // CHUNK
ROT_DIM = HEAD_DIM // 4
ROPE_THETA = 500000.0
ATTN_SCALE = 1.0 / math.sqrt(HEAD_DIM)
NEG_INF = -1e30
D_FF = ((8 * D_MODEL // 3 + 255) // 256) * 256

kernel_name = "yoco_rwkv7_swa_sink_stream_step"


def rmsnorm(x, g):
    xf = x.astype(jnp.float32)
    y = xf * lax.rsqrt(jnp.mean(xf * xf, axis=-1, keepdims=True) + RMS_EPS)
    return (y * g.astype(jnp.float32)).astype(x.dtype)


def swiglu(x, w_in, w_out):
    gu = x @ w_in
    return (jax.nn.silu(gu[..., :D_FF]) * gu[..., D_FF:]) @ w_out


def partial_rope(x, pos):
    half = ROT_DIM // 2
    inv = jnp.power(jnp.float32(ROPE_THETA), -jnp.arange(half, dtype=jnp.float32) * (2.0 / ROT_DIM))
    ang = pos.astype(jnp.float32)[:, None] * inv[None, :]
    cos = jnp.cos(ang)[None, :, None, :]
    sin = jnp.sin(ang)[None, :, None, :]
    xr = x[..., :ROT_DIM].astype(jnp.float32)
    x1, x2 = xr[..., :half], xr[..., half:]
    rot = jnp.concatenate([x1 * cos - x2 * sin, x2 * cos + x1 * sin], axis=-1).astype(x.dtype)
    return jnp.concatenate([rot, x[..., ROT_DIM:]], axis=-1)


def rwkv7_time_mix(x, shift_prev, s0, mu, w_rkv, w0, w1, w2, a0, a1, a2, g1, g2,
                   k_k, k_a, r_k, lnx_w, lnx_b, w_o):
    f32 = jnp.float32
    bsz, t_len, _ = x.shape
    x_prev = jnp.concatenate([shift_prev[:, None, :].astype(x.dtype), x[:, :-1]], axis=1)
    xx = x_prev - x
    xr = x + xx * mu[0]
    xw = x + xx * mu[1]
    xk = x + xx * mu[2]
    xv = x + xx * mu[3]
    xa = x + xx * mu[4]
    xg = x + xx * mu[5]
    r = xr @ w_rkv[0]
    k = xk @ w_rkv[1]
    v = xv @ w_rkv[2]
    w_log = -jax.nn.softplus(-(w0 + jnp.tanh(xw @ w1) @ w2)) - 0.5
    decay = jnp.exp(-jnp.exp(w_log.astype(f32)))
    a = jax.nn.sigmoid(a0 + (xa @ a1) @ a2)
    g = jax.nn.sigmoid(xg @ g1) @ g2

    def heads(t):
        return t.astype(f32).reshape(bsz, t_len, RW_HEADS, RW_HEAD)

    kk = heads(k * k_k)
    kk = kk * lax.rsqrt(jnp.maximum(jnp.sum(kk * kk, axis=-1, keepdims=True), 1e-24))
    k = k * (1 + (a - 1) * k_a)
    rh, kh, vh, wh, ah = heads(r), heads(k), heads(v), heads(decay), heads(a)
    a_vec = -kk
    b_vec = kk * ah

    def step(s, inp):
        r_t, w_t, k_t, v_t, a_t, b_t = inp
        sa = jnp.einsum('bhvk,bhk->bhv', s, a_t)
        s = s * w_t[:, :, None, :] + sa[..., None] * b_t[:, :, None, :] + v_t[..., None] * k_t[:, :, None, :]
        return s, jnp.einsum('bhvk,bhk->bhv', s, r_t)

    seq_in = tuple(jnp.swapaxes(t, 0, 1) for t in (rh, wh, kh, vh, a_vec, b_vec))
    s_fin, y = lax.scan(step, s0.astype(f32), seq_in)
    y = jnp.swapaxes(y, 0, 1)
    yc = y - jnp.mean(y, axis=-1, keepdims=True)
    yn = yc * lax.rsqrt(jnp.mean(yc * yc, axis=-1, keepdims=True) + LNX_EPS)
    yn = yn.reshape(bsz, t_len, D_MODEL) * lnx_w.astype(f32) + lnx_b.astype(f32)
    bonus = (jnp.sum(rh * kh * r_k.astype(f32), axis=-1, keepdims=True) * vh).reshape(bsz, t_len, D_MODEL)
    out = ((yn + bonus) * g.astype(f32)).astype(x.dtype) @ w_o
    return out, x[:, -1], s_fin.astype(s0.dtype)


def shared_kv(h, kv_norm, w_kv, b_kv, pos):
    bsz, t_len, _ = h.shape
    kv = rmsnorm(h, kv_norm) @ w_kv + b_kv
    k = kv[..., :N_KV_HEADS * HEAD_DIM].reshape(bsz, t_len, N_KV_HEADS, HEAD_DIM)
    v = kv[..., N_KV_HEADS * HEAD_DIM:].reshape(bsz, t_len, N_KV_HEADS, HEAD_DIM)
    return partial_rope(k, pos), v


def sink_softmax(scores, sinks):
    sk = sinks.astype(jnp.float32)[:, :, None, None]
    m = jnp.maximum(jnp.max(scores, axis=-1, keepdims=True), sk)
    p = jnp.exp(scores - m)
    return p / (jnp.sum(p, axis=-1, keepdims=True) + jnp.exp(sk - m))


def swa_band_attention(q, k, v, sinks):
    bsz, t_len = q.shape[:2]
    n_chunks = t_len // CHUNK
    qc = q.reshape(bsz, n_chunks, CHUNK, N_KV_HEADS, GROUP, HEAD_DIM)

    def band(t):
        tc = t.reshape(bsz, n_chunks, CHUNK, N_KV_HEADS, HEAD_DIM)
        tp = jnp.pad(tc, ((0, 0), (WIN_CHUNKS, 0), (0, 0), (0, 0), (0, 0)))
        return jnp.concatenate([tp[:, j:j + n_chunks] for j in range(WIN_CHUNKS + 1)], axis=2)

    kb, vb = band(k), band(v)
    s = jnp.einsum('bnqhgd,bnkhd->bnhgqk', qc, kb, preferred_element_type=jnp.float32) * ATTN_SCALE
    key_chunk = (jnp.arange(n_chunks)[:, None] - WIN_CHUNKS
                 + jnp.arange((WIN_CHUNKS + 1) * CHUNK)[None, :] // CHUNK)
    valid = key_chunk >= 0
    s = jnp.where(valid[None, :, None, None, None, :], s, NEG_INF)
    p = sink_softmax(s, sinks)
    o = jnp.einsum('bnhgqk,bnkhd->bnqhgd', p.astype(v.dtype), vb)
    return o.reshape(bsz, t_len, N_Q_HEADS * HEAD_DIM)


def swa_cached_attention(q, k_all, v_all, sinks):
    bsz, t_len = q.shape[:2]
    s = jnp.einsum('bqhgd,bkhd->bhgqk', q, k_all, preferred_element_type=jnp.float32) * ATTN_SCALE
    p = sink_softmax(s, sinks)
    o = jnp.einsum('bhgqk,bkhd->bqhgd', p.astype(v_all.dtype), v_all)
    return o.reshape(bsz, t_len, N_Q_HEADS * HEAD_DIM)


def setup_inputs(seed: int = 0) -> dict:
    key = jax.random.key(seed)
    ks = iter(jax.random.split(key, 48))
    nrm = lambda shape, scale: jax.random.normal(next(ks), shape, jnp.float32) * scale
    gain = lambda shape: 1.0 + nrm(shape, 0.02)
    D = D_MODEL
    KVW = N_KV_HEADS * HEAD_DIM
    QW = N_Q_HEADS * HEAD_DIM
    return {
        "x_prompt": nrm((BATCH, SEQ, D), 1.0),
        "x_sample": nrm((DEC_BATCH, DEC_SEQ, D), 1.0),
        "state_wkv": nrm((N_A, DEC_BATCH, RW_HEADS, RW_HEAD, RW_HEAD), 0.5),
        "state_shift": nrm((N_A, DEC_BATCH, D), 1.0),
        "cache_k": nrm((DEC_BATCH, WINDOW, N_KV_HEADS, HEAD_DIM), 1.0),
        "cache_v": nrm((DEC_BATCH, WINDOW, N_KV_HEADS, HEAD_DIM), 1.0),
        "norm_mix": gain((DEPTH, D)),
        "norm_ffn": gain((DEPTH, D)),
        "rw_mu": jax.random.uniform(next(ks), (N_A, 6, D), jnp.float32),
        "rw_w_rkv": nrm((N_A, 3, D, D), D ** -0.5),
        "rw_w0": -1.5 + nrm((N_A, D), 1.0),
        "rw_w1": nrm((N_A, D, D_DECAY_LORA), D ** -0.5),
        "rw_w2": nrm((N_A, D_DECAY_LORA, D), 0.1 * D_DECAY_LORA ** -0.5),
        "rw_a0": nrm((N_A, D), 0.1),
        "rw_a1": nrm((N_A, D, D_AAA_LORA), D ** -0.5),
        "rw_a2": nrm((N_A, D_AAA_LORA, D), 0.5 * D_AAA_LORA ** -0.5),
        "rw_g1": nrm((N_A, D, D_GATE_LORA), D ** -0.5),
        "rw_g2": nrm((N_A, D_GATE_LORA, D), D_GATE_LORA ** -0.5),
        "rw_k_k": 0.85 + nrm((N_A, D), 0.02),
        "rw_k_a": gain((N_A, D)),
        "rw_r_k": nrm((N_A, RW_HEADS, RW_HEAD), 0.1),
        "rw_lnx_w": gain((N_A, D)),
        "rw_lnx_b": nrm((N_A, D), 0.02),
        "rw_w_o": nrm((N_A, D, D), D ** -0.5),
        "kv_norm": gain((D,)),
        "w_kv": nrm((D, 2 * KVW), D ** -0.5),
        "b_kv": nrm((2 * KVW,), 0.02),
        "w_q": nrm((N_B, D, QW), D ** -0.5),
        "b_q": nrm((N_B, QW), 0.02),
        "attn_sinks": nrm((N_B, N_KV_HEADS, GROUP), 1.0),
        "w_o": nrm((N_B, QW, D), QW ** -0.5),
        "b_o": nrm((N_B, D), 0.02),
        "ffn_w_in": nrm((DEPTH, D, 2 * D_FF), D ** -0.5),
        "ffn_w_out": nrm((DEPTH, D_FF, D), D_FF ** -0.5),
        "norm_final": gain((D,)),
    }


def reference(x_prompt, x_sample, state_wkv, state_shift, cache_k, cache_v,
              norm_mix, norm_ffn, rw_mu, rw_w_rkv, rw_w0, rw_w1, rw_w2, rw_a0, rw_a1, rw_a2,
              rw_g1, rw_g2, rw_k_k, rw_k_a, rw_r_k, rw_lnx_w, rw_lnx_b, rw_w_o,
              kv_norm, w_kv, b_kv, w_q, b_q, attn_sinks, w_o, b_o, ffn_w_in, ffn_w_out, norm_final):

    def run(x, pos, shift0, s0, win_k, win_v):
        bsz, t_len, _ = x.shape
        shifts, states = [], []
        k_all = v_all = k_state = v_state = None
        for l in range(DEPTH):
            h = rmsnorm(x, norm_mix[l])
            if l < N_A:
                h, sh, st = rwkv7_time_mix(h, shift0[l], s0[l], rw_mu[l], rw_w_rkv[l], rw_w0[l], rw_w1[l],
                                           rw_w2[l], rw_a0[l], rw_a1[l], rw_a2[l], rw_g1[l], rw_g2[l],
                                           rw_k_k[l], rw_k_a[l], rw_r_k[l], rw_lnx_w[l], rw_lnx_b[l],
                                           rw_w_o[l])
                shifts.append(sh)
                states.append(st)
            else:
                if l == N_A:
                    k_new, v_new = shared_kv(x, kv_norm, w_kv, b_kv, pos)
                    if win_k is None:
                        k_all, v_all = k_new, v_new
                    else:
                        k_all = jnp.concatenate([win_k.astype(k_new.dtype), k_new], axis=1)
                        v_all = jnp.concatenate([win_v.astype(v_new.dtype), v_new], axis=1)
                    k_state, v_state = k_all[:, -WINDOW:], v_all[:, -WINDOW:]
                j = l - N_A
                q = (h @ w_q[j] + b_q[j]).reshape(bsz, t_len, N_Q_HEADS, HEAD_DIM)
                q = partial_rope(q, pos).reshape(bsz, t_len, N_KV_HEADS, GROUP, HEAD_DIM)
                if win_k is None:
                    o = swa_band_attention(q, k_all, v_all, attn_sinks[j])
                else:
                    o = swa_cached_attention(q, k_all, v_all, attn_sinks[j])
                h = o @ w_o[j] + b_o[j]
            x = x + h
            x = x + swiglu(rmsnorm(x, norm_ffn[l]), ffn_w_in[l], ffn_w_out[l])
        return rmsnorm(x, norm_final), jnp.stack(states), jnp.stack(shifts), k_state, v_state

    bp = x_prompt.shape[0]
    pos_p = jnp.arange(x_prompt.shape[1], dtype=jnp.int32)
    pos_s = PAST_LEN + jnp.arange(x_sample.shape[1], dtype=jnp.int32)
    shift_zero = jnp.zeros((N_A, bp, D_MODEL), x_prompt.dtype)
    wkv_zero = jnp.zeros((N_A, bp, RW_HEADS, RW_HEAD, RW_HEAD), x_prompt.dtype)
    y_prompt, wkv_p, shift_p, k_p, v_p = run(x_prompt, pos_p, shift_zero, wkv_zero, None, None)
    y_sample, wkv_s, shift_s, k_s, v_s = run(x_sample, pos_s, state_shift, state_wkv, cache_k, cache_v)
    return (y_prompt, y_sample, wkv_p, shift_p, k_p, v_p, wkv_s, shift_s, k_s, v_s)
```

```python
import functools

import jax
import jax.numpy as jnp
from jax import lax
from jax.experimental import pallas as pl
from jax.experimental.pallas import tpu as pltpu

F32 = jnp.float32
BF16 = jnp.bfloat16

HEAD = 64
LANES = 128
CHUNK = 64
WINDOW = 128
ROT_DIM = HEAD // 4
ROPE_THETA = 500000.0
RMS_EPS = 1e-5
LNX_EPS = 64e-5
NEG_INF = -1e30
ATTN_SCALE = 1.0 / (HEAD ** 0.5)
VMEM_LIMIT_BYTES = 56 * 1024 * 1024
PAST_LEN = 1024


def _resident(shape, index_map):
    return pl.BlockSpec(shape, index_map, pipeline_mode=pl.Buffered(1))


def _dot(a, b):
    return jnp.dot(a.astype(BF16), b.astype(BF16), preferred_element_type=F32)


def _dot_nt(a, b):
    return lax.dot_general(a.astype(BF16), b.astype(BF16), (((1,), (1,)), ((), ())),
                           preferred_element_type=F32)


def _dot_tn(a, b):
    return lax.dot_general(a.astype(BF16), b.astype(BF16), (((0,), (0,)), ((), ())),
                           preferred_element_type=F32)


def _split3(x):
    p1 = x.astype(BF16)
    r1 = x - p1.astype(F32)
    p2 = r1.astype(BF16)
    p3 = (r1 - p2.astype(F32)).astype(BF16)
    return p1, p2, p3


def _split2(x):
    hi = x.astype(BF16)
    return hi, (x - hi.astype(F32)).astype(BF16)


def _rms(x):
    return x * lax.rsqrt(jnp.mean(x * x, axis=-1, keepdims=True) + RMS_EPS)


def _sigmoid(x):
    return 1.0 / (1.0 + jnp.exp(-x))


def _head_ones2():
    r = lax.broadcasted_iota(jnp.int32, (2 * LANES, LANES), 0)
    c = lax.broadcasted_iota(jnp.int32, (2 * LANES, LANES), 1)
    return jnp.where(((r % LANES) // HEAD) == (c // HEAD), 1.0, 0.0).astype(BF16)


def _head_sum(x, ones2):
    hi, lo = _split2(x)
    return jnp.dot(jnp.concatenate([hi, lo], axis=1), ones2, preferred_element_type=F32)


def _pair_rhs(q, in_a):
    zero = jnp.zeros_like(q)
    return jnp.concatenate([jnp.where(in_a, q, zero), jnp.where(in_a, zero, q)], axis=0).astype(BF16)


def _pairmm(p, q, in_a):
    return jnp.dot(p.astype(BF16), _pair_rhs(q, in_a), preferred_element_type=F32)


def _mix_kernel(x_ref, sh0_ref, gain_ref, mu_ref, wrkv_ref, w0_ref, w1_ref, w2_ref, a0_ref, a1_ref, a2_ref,
                g1_ref, g2_ref, kk_ref, ka_ref,
                r_out, w_out, k_out, v_out, a_out, b_out, g_out, sh_out, carry_ref):
    t = pl.program_id(1)

    @pl.when(t == 0)
    def _():
        carry_ref[...] = sh0_ref[0]

    x = x_ref[0]
    tm, d = x.shape
    h = _rms(x) * gain_ref[...]
    rows = lax.broadcasted_iota(jnp.int32, (tm, 1), 0)
    prev = jnp.where(rows == 0, carry_ref[...], pltpu.roll(h, 1, 0))
    last = h[tm - 1:tm, :]
    carry_ref[...] = last
    sh_out[0] = last

    xx = prev - h
    xr = h + xx * mu_ref[0:1, :]
    xw = h + xx * mu_ref[1:2, :]
    xk = h + xx * mu_ref[2:3, :]
    xv = h + xx * mu_ref[3:4, :]
    xa = h + xx * mu_ref[4:5, :]
    xg = h + xx * mu_ref[5:6, :]

    r = _dot(xr, wrkv_ref[0])
    k = _dot(xk, wrkv_ref[1])
    v = _dot(xv, wrkv_ref[2])
    wl = w0_ref[...] + _dot(jnp.tanh(_dot(xw, w1_ref[...])), w2_ref[...])
    sp = jnp.maximum(-wl, 0.0) + jnp.log(1.0 + jnp.exp(-jnp.abs(wl)))
    logdecay = -jnp.exp(-sp - 0.5)
    a = _sigmoid(a0_ref[...] + _dot(_dot(xa, a1_ref[...]), a2_ref[...]))
    g = _dot(_sigmoid(_dot(xg, g1_ref[...])), g2_ref[...])

    kk = k * kk_ref[...]
    ones2 = _head_ones2()
    kkn = []
    for p in range(d // LANES):
        kp = kk[:, p * LANES:(p + 1) * LANES]
        ss = _head_sum(kp * kp, ones2)
        kkn.append(kp * lax.rsqrt(jnp.maximum(ss, 1e-24)))
    kkn = jnp.concatenate(kkn, axis=1)

    r_out[0] = r
    w_out[0] = logdecay
    k_out[0] = k * (1.0 + (a - 1.0) * ka_ref[...])
    v_out[0] = v
    a_out[0] = -kkn
    b_out[0] = kkn * a
    g_out[0] = g


def _time_mix_proj(x, shift0, gain, mu, wrkv, w0, w1, w2, a0, a1, a2, g1, g2, k_k, k_a, tm):
    bsz, t_len, d = x.shape
    n_t = t_len // tm
    row = lambda arr: arr.reshape(1, -1)
    const2 = lambda shape: _resident(shape, lambda b, t: (0, 0))
    const3 = lambda shape: _resident(shape, lambda b, t: (0, 0, 0))
    tile = pl.BlockSpec((1, tm, d), lambda b, t: (b, t, 0))
    per_b = pl.BlockSpec((1, 1, d), lambda b, t: (b, 0, 0))
    act = jax.ShapeDtypeStruct((bsz, t_len, d), F32)
    outs = pl.pallas_call(
        _mix_kernel,
        grid=(bsz, n_t),
        in_specs=[tile, per_b, const2((1, d)), const2(mu.shape), const3(wrkv.shape), const2((1, d)),
                  const2(w1.shape), const2(w2.shape), const2((1, d)), const2(a1.shape), const2(a2.shape),
                  const2(g1.shape), const2(g2.shape), const2((1, d)), const2((1, d))],
        out_specs=[tile] * 7 + [per_b],
        out_shape=[act] * 7 + [jax.ShapeDtypeStruct((bsz, 1, d), F32)],
        scratch_shapes=[pltpu.VMEM((1, d), F32)],
        compiler_params=pltpu.CompilerParams(dimension_semantics=("arbitrary", "arbitrary"),
                                             vmem_limit_bytes=VMEM_LIMIT_BYTES),
    )(x, shift0.reshape(bsz, 1, d), row(gain), mu, wrkv, row(w0), w1, w2, row(a0), a1, a2, g1, g2,
      row(k_k), row(k_a))
    return outs


def _wkv_kernel(r_ref, w_ref, k_ref, v_ref, a_ref, b_ref, s0_ref, rk_ref, lw_ref, lb_ref,
                z_ref, sout_ref, st_ref, *, n_sub):
    c = pl.program_id(2)

    @pl.when(c == 0)
    def _():
        st_ref[...] = s0_ref[0]

    shape = (CHUNK, LANES)
    row = lax.broadcasted_iota(jnp.int32, shape, 0)
    lane = lax.broadcasted_iota(jnp.int32, shape, 1)
    in_a = lane < HEAD
    li = lane % HEAD
    strict = li < row
    incl = li <= row
    eye2 = jnp.where(li == row, 1.0, 0.0).astype(F32)
    lane4 = lax.broadcasted_iota(jnp.int32, (CHUNK, 4 * CHUNK), 1)
    row4 = lax.broadcasted_iota(jnp.int32, (CHUNK, 4 * CHUNK), 0)
    tri4 = jnp.where(((lane4 % CHUNK) <= row4) & (lane4 < 3 * CHUNK), 1.0, 0.0).astype(BF16)
    ones4 = jnp.ones((4 * CHUNK, LANES), BF16)
    ones2 = _head_ones2()
    zpad = jnp.zeros(shape, BF16)
    rk = rk_ref[0]
    lnw = lw_ref[0]
    lnb = lb_ref[0]

    st = st_ref[...]
    for s in range(n_sub):
        sl = slice(s * CHUNK, (s + 1) * CHUNK)
        r = r_ref[0, sl, :]
        w = w_ref[0, sl, :]
        k = k_ref[0, sl, :]
        v = v_ref[0, sl, :]
        a = a_ref[0, sl, :]
        b = b_ref[0, sl, :]

        w3 = jnp.concatenate(list(_split3(w)) + [zpad], axis=0)
        g = jnp.dot(tri4, w3, preferred_element_type=F32)
        e = jnp.exp(g)
        ei = jnp.exp(-g)
        at = a * e * jnp.exp(-w)
        rt = r * e
        bt = b * ei
        kt = k * ei
        e_last = e[CHUNK - 1:CHUNK, :]
        bdec = bt * e_last
        kdec = kt * e_last

        zero = jnp.zeros_like(bt)
        rhs4 = jnp.concatenate([jnp.where(in_a, bt, zero), jnp.where(in_a, zero, bt),
                                jnp.where(in_a, kt, zero), jnp.where(in_a, zero, kt)], axis=0)
        aa = _dot_nt(jnp.concatenate([at, rt], axis=0), rhs4)
        a_ab = jnp.where(strict, aa[0:CHUNK, 0:LANES], 0.0)
        a_ak = jnp.where(strict, aa[0:CHUNK, LANES:2 * LANES], 0.0)
        a_rb = jnp.where(incl, aa[CHUNK:2 * CHUNK, 0:LANES], 0.0)
        a_rk = jnp.where(incl, aa[CHUNK:2 * CHUNK, LANES:2 * LANES], 0.0)

        xpow = a_ab
        tinv = eye2 + a_ab
        for _ in range(5):
            xpow = _pairmm(xpow, xpow, in_a)
            tinv = tinv + _pairmm(tinv, xpow, in_a)

        at2 = _pairmm(tinv, at, in_a)
        wv = _pairmm(tinv, _pairmm(a_ak, v, in_a), in_a)

        u = _pairmm(at2, st, in_a) + wv
        rhs_y = jnp.concatenate([_pair_rhs(st, in_a), _pair_rhs(u, in_a), _pair_rhs(v, in_a)], axis=0)
        y = jnp.dot(jnp.concatenate([rt, a_rb, a_rk], axis=1).astype(BF16), rhs_y,
                    preferred_element_type=F32)

        upd = _dot_tn(jnp.concatenate([bdec, kdec], axis=0), jnp.concatenate([u, v], axis=0))
        gsum = lax.dot_general(w3, ones4, (((0,), (0,)), ((), ())), preferred_element_type=F32)
        decay_c = jnp.exp(jnp.where(in_a, gsum[0:CHUNK], gsum[CHUNK:2 * CHUNK]))
        st = decay_c * st + jnp.where(in_a, upd[0:CHUNK], upd[CHUNK:2 * CHUNK])

        mean = _head_sum(y, ones2) * (1.0 / HEAD)
        yc = y - mean
        var = _head_sum(yc * yc, ones2) * (1.0 / HEAD)
        yn = yc * lax.rsqrt(var + LNX_EPS) * lnw + lnb
        bonus = _head_sum(r * k * rk, ones2) * v
        z_ref[0, sl, :] = yn + bonus

    st_ref[...] = st

    @pl.when(c == pl.num_programs(2) - 1)
    def _():
        sout_ref[0] = st


def _wkv(r, w, k, v, a, b, st0, r_k, lnx_w, lnx_b, n_sub):
    bsz, t_len, d = r.shape
    n_pairs = d // LANES
    rows = n_sub * CHUNK
    seq = pl.BlockSpec((1, rows, LANES), lambda bi, p, c: (bi, c, p))
    state = pl.BlockSpec((1, HEAD, LANES), lambda bi, p, c: (bi, 0, p))
    vec = pl.BlockSpec((1, 1, LANES), lambda bi, p, c: (p, 0, 0))
    per_pair = lambda arr: arr.reshape(n_pairs, 1, LANES)
    return pl.pallas_call(
        functools.partial(_wkv_kernel, n_sub=n_sub),
        grid=(bsz, n_pairs, t_len // rows),
        in_specs=[seq] * 6 + [state, vec, vec, vec],
        out_specs=[seq, state],
        out_shape=[jax.ShapeDtypeStruct((bsz, t_len, d), F32), jax.ShapeDtypeStruct((bsz, HEAD, d), F32)],
        scratch_shapes=[pltpu.VMEM((HEAD, LANES), F32)],
        compiler_params=pltpu.CompilerParams(dimension_semantics=("arbitrary", "arbitrary", "arbitrary"),
                                             vmem_limit_bytes=VMEM_LIMIT_BYTES),
    )(r, w, k, v, a, b, st0, per_pair(r_k), per_pair(lnx_w), per_pair(lnx_b))


def _swiglu_residual(x, gain, win_ref, wout_ref, ff_chunk):
    n_ff = wout_ref.shape[0]
    hn = (_rms(x) * gain).astype(BF16)
    acc = x
    for j in range(n_ff // ff_chunk):
        lo = j * ff_chunk
        gate = jnp.dot(hn, win_ref[:, lo:lo + ff_chunk], preferred_element_type=F32)
        up = jnp.dot(hn, win_ref[:, n_ff + lo:n_ff + lo + ff_chunk], preferred_element_type=F32)
        hid = gate * _sigmoid(gate) * up
        acc = acc + jnp.dot(hid.astype(BF16), wout_ref[lo:lo + ff_chunk, :], preferred_element_type=F32)
    return acc


def _gate_ffn_kernel(z_ref, g_ref, x_ref, wo_ref, gain_ref, win_ref, wout_ref, o_ref, *, ff_chunk):
    x1 = x_ref[...] + _dot(z_ref[...] * g_ref[...], wo_ref[...])
    o_ref[...] = _swiglu_residual(x1, gain_ref[...], win_ref, wout_ref, ff_chunk)


def _gate_ffn(z, g, x, w_o, gain, w_in, w_out, tm, ff_chunk):
    n, d = x.shape
    tile = pl.BlockSpec((tm, d), lambda i: (i, 0))
    const = lambda arr: _resident(arr.shape, lambda i: (0, 0))
    gain = gain.reshape(1, d)
    return pl.pallas_call(
        functools.partial(_gate_ffn_kernel, ff_chunk=ff_chunk),
        grid=(n // tm,),
        in_specs=[tile, tile, tile, const(w_o), const(gain), const(w_in), const(w_out)],
        out_specs=tile,
        out_shape=jax.ShapeDtypeStruct((n, d), F32),
        compiler_params=pltpu.CompilerParams(dimension_semantics=("arbitrary",),
                                             vmem_limit_bytes=VMEM_LIMIT_BYTES),
    )(z, g, x, w_o, gain, w_in, w_out)


def _rope(x, cs, sn, first):
    out = []
    for p in range(x.shape[1] // LANES):
        xp = x[:, p * LANES:(p + 1) * LANES]
        partner = jnp.where(first, pltpu.roll(xp, LANES - ROT_DIM // 2, 1), pltpu.roll(xp, ROT_DIM // 2, 1))
        out.append(xp * cs + partner * sn)
    return jnp.concatenate(out, axis=1)


def _qkv_kernel(x_ref, gq_ref, gkv_ref, wq_ref, bq_ref, wkv_ref, bkv_ref, inv_ref, q_out, k_out, v_out,
                *, seq_len, pos_base):
    x = x_ref[...]
    tm = x.shape[0]
    kvw = k_out.shape[1]
    y = _rms(x)
    q = _dot(y * gq_ref[...], wq_ref[...]) + bq_ref[...]
    kv = _dot(y * gkv_ref[...], wkv_ref[...]) + bkv_ref[...]

    rows = lax.broadcasted_iota(jnp.int32, (tm, 1), 0) + pl.program_id(0) * tm
    pos = (lax.rem(rows, seq_len) + pos_base).astype(F32)
    ang = pos * inv_ref[...]
    lane = lax.broadcasted_iota(jnp.int32, (1, LANES), 1) % HEAD
    first = lane < ROT_DIM // 2
    cs = jnp.cos(ang)
    sn = jnp.where(first, -1.0, 1.0) * jnp.sin(ang)
    q_out[...] = _rope(q, cs, sn, first)
    k_out[...] = _rope(kv[:, :kvw], cs, sn, first)
    v_out[...] = kv[:, kvw:]


def _qkv_proj(x, gain_q, gain_kv, w_q, b_q, w_kv, b_kv, seq_len, pos_base, tm):
    n, d = x.shape
    kvw = w_kv.shape[1] // 2
    half = ROT_DIM // 2
    lane = jnp.arange(LANES) % HEAD
    inv = jnp.power(jnp.float32(ROPE_THETA), -(lane % half).astype(F32) * (2.0 / ROT_DIM))
    inv = jnp.where(lane < ROT_DIM, inv, 0.0).reshape(1, LANES)
    tile = lambda w: pl.BlockSpec((tm, w), lambda i: (i, 0))
    const = lambda arr: _resident(arr.shape, lambda i: (0, 0))
    args = (x, gain_q.reshape(1, d), gain_kv.reshape(1, d), w_q, b_q.reshape(1, -1), w_kv,
            b_kv.reshape(1, -1), inv)
    return pl.pallas_call(
        functools.partial(_qkv_kernel, seq_len=seq_len, pos_base=pos_base),
        grid=(n // tm,),
        in_specs=[tile(d)] + [const(a) for a in args[1:]],
        out_specs=[tile(w_q.shape[1]), tile(kvw), tile(kvw)],
        out_shape=[jax.ShapeDtypeStruct((n, w_q.shape[1]), F32), jax.ShapeDtypeStruct((n, kvw), F32),
                   jax.ShapeDtypeStruct((n, kvw), F32)],
        compiler_params=pltpu.CompilerParams(dimension_semantics=("arbitrary",),
                                             vmem_limit_bytes=VMEM_LIMIT_BYTES),
    )(*args)


def _dup_heads(kp, in_a):
    rolled = pltpu.roll(kp, HEAD, 1)
    return jnp.where(in_a, kp, rolled), jnp.where(in_a, rolled, kp)


def _attend_chunk(q, keys, vals, sink_ref, valid):
    n_kv = len(keys)
    groups_per_kv = q.shape[1] // LANES // n_kv
    in_a = lax.broadcasted_iota(jnp.int32, (CHUNK, LANES), 1) < HEAD
    n_keys = keys[0].shape[0]
    ones = jnp.ones((n_keys, LANES), BF16)
    out = []
    for hk in range(n_kv):
        blocks, sinks = [], []
        for m in range(groups_per_kv):
            qg = q[:, (hk * groups_per_kv + m) * LANES:(hk * groups_per_kv + m + 1) * LANES]
            zero = jnp.zeros_like(qg)
            blocks += [jnp.where(in_a, qg, zero), jnp.where(in_a, zero, qg)]
            for half in range(2):
                sinks.append(jnp.full((CHUNK, 1), sink_ref[(hk * groups_per_kv + m) * 2 + half], F32))
        s = _dot_nt(jnp.concatenate(blocks, axis=0), keys[hk]) * ATTN_SCALE
        if valid is not None:
            s = jnp.where(valid, s, NEG_INF)
        sk = jnp.concatenate(sinks, axis=0)
        m_row = jnp.maximum(jnp.max(s, axis=-1, keepdims=True), sk)
        p = jnp.exp(s - m_row).astype(BF16)
        pv = jnp.dot(p, jnp.concatenate([vals[hk].astype(BF16), ones], axis=1), preferred_element_type=F32)
        o = pv[:, :LANES] / (pv[:, LANES:] + jnp.exp(sk - m_row))
        for m in range(groups_per_kv):
            out.append(jnp.where(in_a, o[2 * m * CHUNK:(2 * m + 1) * CHUNK],
                                 o[(2 * m + 1) * CHUNK:(2 * m + 2) * CHUNK]))
    return jnp.concatenate(out, axis=1)


def _attn_ffn_kernel(sink_ref, q_ref, kc_ref, vc_ref, kp_ref, vp_ref, x_ref, wo_ref, bo_ref, gain_ref,
                     win_ref, wout_ref, gfin_ref, o_ref, *, streaming, ff_chunk):
    tm = x_ref.shape[-2]
    n_chunks = tm // CHUNK
    kvw = kc_ref.shape[-1]
    n_kv = kvw // HEAD
    n_keys = WINDOW + CHUNK

    def dup(arr):
        in_a = lax.broadcasted_iota(jnp.int32, (arr.shape[0], LANES), 1) < HEAD
        res = []
        for j in range(kvw // LANES):
            res += list(_dup_heads(arr[:, j * LANES:(j + 1) * LANES], in_a))
        return res

    if streaming:
        q = q_ref[0]
        k_all = dup(jnp.concatenate([kp_ref[0], kc_ref[0]], axis=0))
        v_all = dup(jnp.concatenate([vp_ref[0], vc_ref[0]], axis=0))
        first_tile = pl.program_id(1) == 0
        key_idx = lax.broadcasted_iota(jnp.int32, (1, n_keys), 1)
    else:
        q = q_ref[...]

    outs = []
    for c in range(n_chunks):
        qc = q[c * CHUNK:(c + 1) * CHUNK]
        valid = None
        if streaming:
            keys = [ka[c * CHUNK:c * CHUNK + n_keys] for ka in k_all]
            vals = [va[c * CHUNK:c * CHUNK + n_keys] for va in v_all]
            if c * CHUNK < WINDOW:
                valid = jnp.logical_or(key_idx + c * CHUNK >= WINDOW, jnp.logical_not(first_tile))
        else:
            keys = dup(jnp.concatenate([kp_ref[c], kc_ref[c * CHUNK:(c + 1) * CHUNK]], axis=0))
            vals = dup(jnp.concatenate([vp_ref[c], vc_ref[c * CHUNK:(c + 1) * CHUNK]], axis=0))
        outs.append(_attend_chunk(qc, keys, vals, sink_ref, valid))
    o = jnp.concatenate(outs, axis=0)

    x = x_ref[0] if streaming else x_ref[...]
    x1 = x + _dot(o, wo_ref[...]) + bo_ref[...]
    x2 = _swiglu_residual(x1, gain_ref[...], win_ref, wout_ref, ff_chunk)
    y = _rms(x2) * gfin_ref[...]
    if streaming:
        o_ref[0] = y
    else:
        o_ref[...] = y


def _attn_ffn(q, k_new, v_new, k_prev, v_prev, x, sinks, w_o, b_o, gain, w_in, w_out, gain_fin, tm, ff_chunk):
    d = x.shape[-1]
    kvw = k_new.shape[-1]
    streaming = k_prev is None
    smem = pl.BlockSpec(memory_space=pltpu.SMEM)
    weights = (w_o, b_o.reshape(1, d), gain.reshape(1, d), w_in, w_out, gain_fin.reshape(1, d))
    if streaming:
        bsz, t_len, _ = x.shape
        per_win = tm // WINDOW
        grid = (bsz, t_len // tm)
        tile = lambda w: pl.BlockSpec((1, tm, w), lambda b, t: (b, t, 0))
        prev = pl.BlockSpec((1, WINDOW, kvw), lambda b, t: (b, jnp.maximum(t * per_win - 1, 0), 0))
        const = lambda arr: _resident(arr.shape, lambda b, t: (0, 0))
        in_specs = [smem, tile(d), tile(kvw), tile(kvw), prev, prev, tile(d)] + [const(a) for a in weights]
        args = (sinks, q, k_new, v_new, k_new, v_new, x) + weights
        out_spec = tile(d)
        sem = ("arbitrary", "arbitrary")
    else:
        n = x.shape[0]
        grid = (1,)
        full = lambda arr: pl.BlockSpec(arr.shape, lambda i: (0,) * arr.ndim)
        args = (sinks, q, k_new, v_new, k_prev, v_prev, x) + weights
        in_specs = [smem] + [full(a) for a in args[1:]]
        out_spec = pl.BlockSpec((n, d), lambda i: (0, 0))
        sem = ("arbitrary",)
    return pl.pallas_call(
        functools.partial(_attn_ffn_kernel, streaming=streaming, ff_chunk=ff_chunk),
        grid=grid,
        in_specs=in_specs,
        out_specs=out_spec,
        out_shape=jax.ShapeDtypeStruct(x.shape, F32),
        compiler_params=pltpu.CompilerParams(dimension_semantics=sem, vmem_limit_bytes=VMEM_LIMIT_BYTES),
    )(*args)


def _tile(n, cap):
    t = min(n, cap)
    while n % t:
        t -= 1
    return t


def kernel(x_prompt, x_sample, state_wkv, state_shift, cache_k, cache_v, norm_mix, norm_ffn, rw_mu, rw_w_rkv, rw_w0, rw_w1, rw_w2, rw_a0, rw_a1, rw_a2, rw_g1, rw_g2, rw_k_k, rw_k_a, rw_r_k, rw_lnx_w, rw_lnx_b, rw_w_o, kv_norm, w_kv, b_kv, w_q, b_q, attn_sinks, w_o, b_o, ffn_w_in, ffn_w_out, norm_final):
    depth, d = norm_mix.shape
    assert depth == 2 and rw_mu.shape[0] == 1 and w_q.shape[0] == 1, "one RWKV layer followed by one attention layer"
    n_heads = d // HEAD
    n_ff = ffn_w_out.shape[1]
    ff_chunk = 256 if n_ff % 256 == 0 else n_ff
    bf = lambda w: w.astype(BF16)
    wrkv, w1, w2, a1, a2, g1, g2 = map(bf, (rw_w_rkv[0], rw_w1[0], rw_w2[0], rw_a1[0], rw_a2[0], rw_g1[0], rw_g2[0]))
    rw_wo, wq, wkv, wo = bf(rw_w_o[0]), bf(w_q[0]), bf(w_kv), bf(w_o[0])
    win, wout = bf(ffn_w_in), bf(ffn_w_out)
    sinks = attn_sinks[0].reshape(-1)

    def run(x, pos_base, shift0, st0, win_k, win_v):
        bsz, t_len, _ = x.shape
        n_rows = bsz * t_len
        r, w, k, v, a, b, g, shift = _time_mix_proj(
            x, shift0, norm_mix[0], rw_mu[0], wrkv, rw_w0[0], w1, w2, rw_a0[0], a1, a2, g1, g2,
            rw_k_k[0], rw_k_a[0], tm=_tile(t_len, 256))
        z, st = _wkv(r, w, k, v, a, b, st0, rw_r_k[0].reshape(-1), rw_lnx_w[0], rw_lnx_b[0],
                     n_sub=_tile(t_len // CHUNK, 8))
        flat = lambda t: t.reshape(n_rows, t.shape[-1])
        x1 = _gate_ffn(flat(z), flat(g), flat(x), rw_wo, norm_ffn[0], win[0], wout[0],
                       tm=_tile(n_rows, 512), ff_chunk=ff_chunk)
        q, k_new, v_new = _qkv_proj(x1, norm_mix[1], kv_norm, wq, b_q[0], wkv, b_kv, t_len, pos_base,
                                    tm=_tile(n_rows, 512))
        kvw = k_new.shape[-1]
        k_new = k_new.reshape(bsz, t_len, kvw)
        v_new = v_new.reshape(bsz, t_len, kvw)
        if win_k is None:
            y = _attn_ffn(q.reshape(bsz, t_len, d), k_new, v_new, None, None, x1.reshape(bsz, t_len, d), sinks,
                          wo, b_o[0], norm_ffn[1], win[1], wout[1], norm_final, tm=_tile(t_len, 512),
                          ff_chunk=ff_chunk)
            k_state, v_state = k_new[:, -WINDOW:], v_new[:, -WINDOW:]
        else:
            assert t_len == CHUNK
            kp = win_k.reshape(bsz, WINDOW, kvw)
            vp = win_v.reshape(bsz, WINDOW, kvw)
            y = _attn_ffn(q, k_new.reshape(n_rows, kvw), v_new.reshape(n_rows, kvw), kp, vp, x1, sinks,
                          wo, b_o[0], norm_ffn[1], win[1], wout[1], norm_final, tm=n_rows, ff_chunk=ff_chunk)
            k_state = jnp.concatenate([kp, k_new], axis=1)[:, -WINDOW:]
            v_state = jnp.concatenate([vp, v_new], axis=1)[:, -WINDOW:]
        n_kv = kvw // HEAD
        st = st.reshape(bsz, HEAD, n_heads, HEAD).transpose(0, 2, 3, 1)[None]
        return (y.reshape(bsz, t_len, d), st, shift.reshape(1, bsz, d),
                k_state.reshape(bsz, WINDOW, n_kv, HEAD), v_state.reshape(bsz, WINDOW, n_kv, HEAD))

    bp = x_prompt.shape[0]
    zeros_state = jnp.zeros((bp, HEAD, d), F32)
    y_p, wkv_p, shift_p, k_p, v_p = run(x_prompt, 0, jnp.zeros((bp, d), F32), zeros_state, None, None)
    st_s = state_wkv[0].transpose(0, 3, 1, 2).reshape(x_sample.shape[0], HEAD, d)
    y_s, wkv_s, shift_s, k_s, v_s = run(x_sample, PAST_LEN, state_shift[0], st_s, cache_k, cache_v)
    return (y_p, y_s, wkv_p, shift_p, k_p, v_p, wkv_s, shift_s, k_s, v_s)
```

```python
import functools

import jax
import jax.numpy as jnp
from jax import lax
from jax.experimental import pallas as pl
from jax.experimental.pallas import tpu as pltpu

F32 = jnp.float32
BF16 = jnp.bfloat16

HEAD = 64
LANES = 128
CHUNK = 64
WINDOW = 128
ROT_DIM = HEAD // 4
ROPE_THETA = 500000.0
RMS_EPS = 1e-5
LNX_EPS = 64e-5
NEG_INF = -1e30
ATTN_SCALE = 1.0 / (HEAD ** 0.5)
VMEM_LIMIT_BYTES = 56 * 1024 * 1024
PAST_LEN = 1024


def _resident(shape, index_map):
    return pl.BlockSpec(shape, index_map, pipeline_mode=pl.Buffered(1))


def _dot(a, b):
    return jnp.dot(a.astype(BF16), b.astype(BF16), preferred_element_type=F32)


def _dot_nt(a, b):
    return lax.dot_general(a.astype(BF16), b.astype(BF16), (((1,), (1,)), ((), ())),
                           preferred_element_type=F32)


def _dot_tn(a, b):
    return lax.dot_general(a.astype(BF16), b.astype(BF16), (((0,), (0,)), ((), ())),
                           preferred_element_type=F32)


def _split3(x):
    p1 = x.astype(BF16)
    r1 = x - p1.astype(F32)
    p2 = r1.astype(BF16)
    p3 = (r1 - p2.astype(F32)).astype(BF16)
    return p1, p2, p3


def _split2(x):
    hi = x.astype(BF16)
    return hi, (x - hi.astype(F32)).astype(BF16)


def _rms(x):
    return x * lax.rsqrt(jnp.mean(x * x, axis=-1, keepdims=True) + RMS_EPS)


def _sigmoid(x):
    return 1.0 / (1.0 + jnp.exp(-x))


def _head_ones2():
    r = lax.broadcasted_iota(jnp.int32, (2 * LANES, LANES), 0)
    c = lax.broadcasted_iota(jnp.int32, (2 * LANES, LANES), 1)
    return jnp.where(((r % LANES) // HEAD) == (c // HEAD), 1.0, 0.0).astype(BF16)


def _head_sum(x, ones2):
    hi, lo = _split2(x)
    return jnp.dot(jnp.concatenate([hi, lo], axis=1), ones2, preferred_element_type=F32)


def _pair_rhs(q, in_a):
    zero = jnp.zeros_like(q)
    return jnp.concatenate([jnp.where(in_a, q, zero), jnp.where(in_a, zero, q)], axis=0).astype(BF16)


def _pairmm(p, q, in_a):
    return jnp.dot(p.astype(BF16), _pair_rhs(q, in_a), preferred_element_type=F32)


def _mix_kernel(x_ref, sh0_ref, gain_ref, mu_ref, wrkv_ref, w0_ref, w1_ref, w2_ref, a0_ref, a1_ref, a2_ref,
                g1_ref, g2_ref, kk_ref, ka_ref,
                r_out, w_out, k_out, v_out, a_out, b_out, g_out, sh_out, carry_ref):
    t = pl.program_id(1)

    @pl.when(t == 0)
    def _():
        carry_ref[...] = sh0_ref[0]

    x = x_ref[0]
    tm, d = x.shape
    h = _rms(x) * gain_ref[...]
    rows = lax.broadcasted_iota(jnp.int32, (tm, 1), 0)
    prev = jnp.where(rows == 0, carry_ref[...], pltpu.roll(h, 1, 0))
    last = h[tm - 1:tm, :]
    carry_ref[...] = last
    sh_out[0] = last

    xx = prev - h
    xr = h + xx * mu_ref[0:1, :]
    xw = h + xx * mu_ref[1:2, :]
    xk = h + xx * mu_ref[2:3, :]
    xv = h + xx * mu_ref[3:4, :]
    xa = h + xx * mu_ref[4:5, :]
    xg = h + xx * mu_ref[5:6, :]

    r = _dot(xr, wrkv_ref[0])
    k = _dot(xk, wrkv_ref[1])
    v = _dot(xv, wrkv_ref[2])
    wl = w0_ref[...] + _dot(jnp.tanh(_dot(xw, w1_ref[...])), w2_ref[...])
    sp = jnp.maximum(-wl, 0.0) + jnp.log(1.0 + jnp.exp(-jnp.abs(wl)))
    logdecay = -jnp.exp(-sp - 0.5)
    a = _sigmoid(a0_ref[...] + _dot(_dot(xa, a1_ref[...]), a2_ref[...]))
    g = _dot(_sigmoid(_dot(xg, g1_ref[...])), g2_ref[...])

    kk = k * kk_ref[...]
    ones2 = _head_ones2()
    kkn = []
    for p in range(d // LANES):
        kp = kk[:, p * LANES:(p + 1) * LANES]
        ss = _head_sum(kp * kp, ones2)
        kkn.append(kp * lax.rsqrt(jnp.maximum(ss, 1e-24)))
    kkn = jnp.concatenate(kkn, axis=1)

    r_out[0] = r
    w_out[0] = logdecay
    k_out[0] = k * (1.0 + (a - 1.0) * ka_ref[...])
    v_out[0] = v
    a_out[0] = -kkn
    b_out[0] = kkn * a
    g_out[0] = g


def _time_mix_proj(x, shift0, gain, mu, wrkv, w0, w1, w2, a0, a1, a2, g1, g2, k_k, k_a, tm):
    bsz, t_len, d = x.shape
    n_t = t_len // tm
    row = lambda arr: arr.reshape(1, -1)
    const2 = lambda shape: _resident(shape, lambda b, t: (0, 0))
    const3 = lambda shape: _resident(shape, lambda b, t: (0, 0, 0))
    tile = pl.BlockSpec((1, tm, d), lambda b, t: (b, t, 0))
    per_b = pl.BlockSpec((1, 1, d), lambda b, t: (b, 0, 0))
    act = jax.ShapeDtypeStruct((bsz, t_len, d), F32)
    outs = pl.pallas_call(
        _mix_kernel,
        grid=(bsz, n_t),
        in_specs=[tile, per_b, const2((1, d)), const2(mu.shape), const3(wrkv.shape), const2((1, d)),
                  const2(w1.shape), const2(w2.shape), const2((1, d)), const2(a1.shape), const2(a2.shape),
                  const2(g1.shape), const2(g2.shape), const2((1, d)), const2((1, d))],
        out_specs=[tile] * 7 + [per_b],
        out_shape=[act] * 7 + [jax.ShapeDtypeStruct((bsz, 1, d), F32)],
        scratch_shapes=[pltpu.VMEM((1, d), F32)],
        compiler_params=pltpu.CompilerParams(dimension_semantics=("arbitrary", "arbitrary"),
                                             vmem_limit_bytes=VMEM_LIMIT_BYTES),
    )(x, shift0.reshape(bsz, 1, d), row(gain), mu, wrkv, row(w0), w1, w2, row(a0), a1, a2, g1, g2,
      row(k_k), row(k_a))
    return outs


def _wkv_kernel(r_ref, w_ref, k_ref, v_ref, a_ref, b_ref, s0_ref, rk_ref, lw_ref, lb_ref,
                z_ref, sout_ref, st_ref, *, n_sub):
    c = pl.program_id(2)

    @pl.when(c == 0)
    def _():
        st_ref[...] = s0_ref[0]

    shape = (CHUNK, LANES)
    row = lax.broadcasted_iota(jnp.int32, shape, 0)
    lane = lax.broadcasted_iota(jnp.int32, shape, 1)
    in_a = lane < HEAD
    li = lane % HEAD
    strict = li < row
    incl = li <= row
    eye2 = jnp.where(li == row, 1.0, 0.0).astype(F32)
    lane4 = lax.broadcasted_iota(jnp.int32, (CHUNK, 4 * CHUNK), 1)
    row4 = lax.broadcasted_iota(jnp.int32, (CHUNK, 4 * CHUNK), 0)
    tri4 = jnp.where(((lane4 % CHUNK) <= row4) & (lane4 < 3 * CHUNK), 1.0, 0.0).astype(BF16)
    ones2 = _head_ones2()
    zpad = jnp.zeros(shape, BF16)
    n_pp = st_ref.shape[1] // LANES
    items = [(s, p) for s in range(n_sub) for p in range(n_pp)]

    def ld(ref):
        return [ref[0, s * CHUNK:(s + 1) * CHUNK, p * LANES:(p + 1) * LANES] for s, p in items]

    def each(fn, *cols):
        return [fn(*xs) for xs in zip(*cols)]

    r, w, k, v, a, b = ld(r_ref), ld(w_ref), ld(k_ref), ld(v_ref), ld(a_ref), ld(b_ref)

    g = each(lambda x: jnp.dot(tri4, jnp.concatenate(list(_split3(x)) + [zpad], axis=0),
                               preferred_element_type=F32), w)
    e = each(jnp.exp, g)
    ei = each(lambda x: jnp.exp(-x), g)
    at = each(lambda a_, e_, w_: a_ * e_ * jnp.exp(-w_), a, e, w)
    rt = each(lambda r_, e_: r_ * e_, r, e)
    bt = each(lambda b_, x: b_ * x, b, ei)
    kt = each(lambda k_, x: k_ * x, k, ei)
    e_last = each(lambda e_: e_[CHUNK - 1:CHUNK, :], e)
    bdec = each(lambda x, el: x * el, bt, e_last)
    kdec = each(lambda x, el: x * el, kt, e_last)

    def pair_products(at_, rt_, bt_, kt_):
        zero = jnp.zeros_like(bt_)
        rhs4 = jnp.concatenate([jnp.where(in_a, bt_, zero), jnp.where(in_a, zero, bt_),
                                jnp.where(in_a, kt_, zero), jnp.where(in_a, zero, kt_)], axis=0)
        return _dot_nt(jnp.concatenate([at_, rt_], axis=0), rhs4)

    aa = each(pair_products, at, rt, bt, kt)
    a_ab = each(lambda x: jnp.where(strict, x[0:CHUNK, 0:LANES], 0.0), aa)
    a_ak = each(lambda x: jnp.where(strict, x[0:CHUNK, LANES:2 * LANES], 0.0), aa)
    a_rb = each(lambda x: jnp.where(incl, x[CHUNK:2 * CHUNK, 0:LANES], 0.0), aa)
    a_rk = each(lambda x: jnp.where(incl, x[CHUNK:2 * CHUNK, LANES:2 * LANES], 0.0), aa)

    mm = lambda p_, q_: _pairmm(p_, q_, in_a)
    xpow = a_ab
    tinv = each(lambda x: eye2 + x, a_ab)
    for _ in range(5):
        xpow = each(mm, xpow, xpow)
        tinv = each(lambda t_, x: t_ + mm(t_, x), tinv, xpow)

    at2 = each(mm, tinv, at)
    wv = each(mm, tinv, each(mm, a_ak, v))
    pn = each(lambda bd, kd, at2_, wv_, v_: _dot_tn(
        jnp.concatenate([bd, kd], axis=0),
        jnp.concatenate([jnp.concatenate([at2_, wv_], axis=1),
                         jnp.concatenate([jnp.zeros_like(v_), v_], axis=1)], axis=0)), bdec, kdec, at2, wv, v)
    pick = lambda x: jnp.where(in_a, x[0:CHUNK], x[CHUNK:2 * CHUNK])
    p_mat = each(lambda x: pick(x[:, 0:LANES]), pn)
    n_mat = each(lambda x: pick(x[:, LANES:2 * LANES]), pn)
    q_mat = each(lambda rt_, arb, at2_: rt_ + mm(arb, at2_), rt, a_rb, at2)
    y0 = each(lambda arb, ark, wv_, v_: jnp.dot(
        jnp.concatenate([arb, ark], axis=1).astype(BF16),
        jnp.concatenate([_pair_rhs(wv_, in_a), _pair_rhs(v_, in_a)], axis=0), preferred_element_type=F32),
        a_rb, a_rk, wv, v)
    decay = each(lambda el: _head_sum(eye2 * el, ones2), e_last)

    st = [st_ref[:, p * LANES:(p + 1) * LANES] for p in range(n_pp)]
    y = []
    for i, (s, p) in enumerate(items):
        res = jnp.dot(jnp.concatenate([q_mat[i], p_mat[i]], axis=0).astype(BF16), _pair_rhs(st[p], in_a),
                      preferred_element_type=F32)
        y.append(res[0:CHUNK] + y0[i])
        st[p] = decay[i] * st[p] + res[CHUNK:2 * CHUNK] + n_mat[i]

    mean = each(lambda y_: _head_sum(y_, ones2) * (1.0 / HEAD), y)
    yc = each(lambda y_, m_: y_ - m_, y, mean)
    var = each(lambda x: _head_sum(x * x, ones2) * (1.0 / HEAD), yc)
    bonus = each(lambda r_, k_, sp: _head_sum(r_ * k_ * rk_ref[0, :, sp[1] * LANES:(sp[1] + 1) * LANES], ones2),
                 r, k, items)
    for i, (s, p) in enumerate(items):
        lanes = slice(p * LANES, (p + 1) * LANES)
        yn = yc[i] * lax.rsqrt(var[i] + LNX_EPS) * lw_ref[0, :, lanes] + lb_ref[0, :, lanes]
        z_ref[0, s * CHUNK:(s + 1) * CHUNK, lanes] = yn + bonus[i] * v[i]

    for p in range(n_pp):
        st_ref[:, p * LANES:(p + 1) * LANES] = st[p]

    @pl.when(c == pl.num_programs(2) - 1)
    def _():
        for p in range(n_pp):
            sout_ref[0, :, p * LANES:(p + 1) * LANES] = st[p]


def _wkv(r, w, k, v, a, b, st0, r_k, lnx_w, lnx_b, n_sub, n_pp):
    bsz, t_len, d = r.shape
    width = n_pp * LANES
    rows = n_sub * CHUNK
    seq = pl.BlockSpec((1, rows, width), lambda bi, p, c: (bi, c, p))
    state = pl.BlockSpec((1, HEAD, width), lambda bi, p, c: (bi, 0, p))
    vec = pl.BlockSpec((1, 1, width), lambda bi, p, c: (p, 0, 0))
    per_group = lambda arr: arr.reshape(d // width, 1, width)
    return pl.pallas_call(
        functools.partial(_wkv_kernel, n_sub=n_sub),
        grid=(bsz, d // width, t_len // rows),
        in_specs=[seq] * 6 + [state, vec, vec, vec],
        out_specs=[seq, state],
        out_shape=[jax.ShapeDtypeStruct((bsz, t_len, d), F32), jax.ShapeDtypeStruct((bsz, HEAD, d), F32)],
        scratch_shapes=[pltpu.VMEM((HEAD, width), F32)],
        compiler_params=pltpu.CompilerParams(dimension_semantics=("arbitrary", "arbitrary", "arbitrary"),
                                             vmem_limit_bytes=VMEM_LIMIT_BYTES),
    )(r, w, k, v, a, b, st0, per_group(r_k), per_group(lnx_w), per_group(lnx_b))


def _swiglu_residual(x, gain, win_ref, wout_ref, ff_chunk):
    n_ff = wout_ref.shape[0]
    hn = (_rms(x) * gain).astype(BF16)
    acc = x
    for j in range(n_ff // ff_chunk):
        lo = j * ff_chunk
        gate = jnp.dot(hn, win_ref[:, lo:lo + ff_chunk], preferred_element_type=F32)
        up = jnp.dot(hn, win_ref[:, n_ff + lo:n_ff + lo + ff_chunk], preferred_element_type=F32)
        hid = gate * _sigmoid(gate) * up
        acc = acc + jnp.dot(hid.astype(BF16), wout_ref[lo:lo + ff_chunk, :], preferred_element_type=F32)
    return acc


def _gate_ffn_kernel(z_ref, g_ref, x_ref, wo_ref, gain_ref, win_ref, wout_ref, o_ref, *, ff_chunk):
    x1 = x_ref[...] + _dot(z_ref[...] * g_ref[...], wo_ref[...])
    o_ref[...] = _swiglu_residual(x1, gain_ref[...], win_ref, wout_ref, ff_chunk)


def _gate_ffn(z, g, x, w_o, gain, w_in, w_out, tm, ff_chunk):
    n, d = x.shape
    tile = pl.BlockSpec((tm, d), lambda i: (i, 0))
    const = lambda arr: _resident(arr.shape, lambda i: (0, 0))
    gain = gain.reshape(1, d)
    return pl.pallas_call(
        functools.partial(_gate_ffn_kernel, ff_chunk=ff_chunk),
        grid=(n // tm,),
        in_specs=[tile, tile, tile, const(w_o), const(gain), const(w_in), const(w_out)],
        out_specs=tile,
        out_shape=jax.ShapeDtypeStruct((n, d), F32),
        compiler_params=pltpu.CompilerParams(dimension_semantics=("arbitrary",),
                                             vmem_limit_bytes=VMEM_LIMIT_BYTES),
    )(z, g, x, w_o, gain, w_in, w_out)


def _rope(x, cs, sn, first):
    out = []
    for p in range(x.shape[1] // LANES):
        xp = x[:, p * LANES:(p + 1) * LANES]
        partner = jnp.where(first, pltpu.roll(xp, LANES - ROT_DIM // 2, 1), pltpu.roll(xp, ROT_DIM // 2, 1))
        out.append(xp * cs + partner * sn)
    return jnp.concatenate(out, axis=1)


def _qkv_kernel(x_ref, gq_ref, gkv_ref, wq_ref, bq_ref, wkv_ref, bkv_ref, inv_ref, q_out, k_out, v_out,
                *, seq_len, pos_base):
    x = x_ref[...]
    tm = x.shape[0]
    kvw = k_out.shape[1]
    y = _rms(x)
    q = _dot(y * gq_ref[...], wq_ref[...]) + bq_ref[...]
    kv = _dot(y * gkv_ref[...], wkv_ref[...]) + bkv_ref[...]

    rows = lax.broadcasted_iota(jnp.int32, (tm, 1), 0) + pl.program_id(0) * tm
    pos = (lax.rem(rows, seq_len) + pos_base).astype(F32)
    ang = pos * inv_ref[...]
    lane = lax.broadcasted_iota(jnp.int32, (1, LANES), 1) % HEAD
    first = lane < ROT_DIM // 2
    cs = jnp.cos(ang)
    sn = jnp.where(first, -1.0, 1.0) * jnp.sin(ang)
    q_out[...] = _rope(q, cs, sn, first)
    k_out[...] = _rope(kv[:, :kvw], cs, sn, first)
    v_out[...] = kv[:, kvw:]


def _qkv_proj(x, gain_q, gain_kv, w_q, b_q, w_kv, b_kv, seq_len, pos_base, tm):
    n, d = x.shape
    kvw = w_kv.shape[1] // 2
    half = ROT_DIM // 2
    lane = jnp.arange(LANES) % HEAD
    inv = jnp.power(jnp.float32(ROPE_THETA), -(lane % half).astype(F32) * (2.0 / ROT_DIM))
    inv = jnp.where(lane < ROT_DIM, inv, 0.0).reshape(1, LANES)
    tile = lambda w: pl.BlockSpec((tm, w), lambda i: (i, 0))
    const = lambda arr: _resident(arr.shape, lambda i: (0, 0))
    args = (x, gain_q.reshape(1, d), gain_kv.reshape(1, d), w_q, b_q.reshape(1, -1), w_kv,
            b_kv.reshape(1, -1), inv)
    return pl.pallas_call(
        functools.partial(_qkv_kernel, seq_len=seq_len, pos_base=pos_base),
        grid=(n // tm,),
        in_specs=[tile(d)] + [const(a) for a in args[1:]],
        out_specs=[tile(w_q.shape[1]), tile(kvw), tile(kvw)],
        out_shape=[jax.ShapeDtypeStruct((n, w_q.shape[1]), F32), jax.ShapeDtypeStruct((n, kvw), F32),
                   jax.ShapeDtypeStruct((n, kvw), F32)],
        compiler_params=pltpu.CompilerParams(dimension_semantics=("arbitrary",),
                                             vmem_limit_bytes=VMEM_LIMIT_BYTES),
    )(*args)


def _dup_heads(kp, in_a):
    rolled = pltpu.roll(kp, HEAD, 1)
    return jnp.where(in_a, kp, rolled), jnp.where(in_a, rolled, kp)


def _attend_chunk(q, keys, vals, sink_ref, valid):
    n_kv = len(keys)
    groups_per_kv = q.shape[1] // LANES // n_kv
    in_a = lax.broadcasted_iota(jnp.int32, (CHUNK, LANES), 1) < HEAD
    n_keys = keys[0].shape[0]
    ones = jnp.ones((n_keys, LANES), BF16)
    out = []
    for hk in range(n_kv):
        blocks, sinks = [], []
        for m in range(groups_per_kv):
            qg = q[:, (hk * groups_per_kv + m) * LANES:(hk * groups_per_kv + m + 1) * LANES]
            zero = jnp.zeros_like(qg)
            blocks += [jnp.where(in_a, qg, zero), jnp.where(in_a, zero, qg)]
            for half in range(2):
                sinks.append(jnp.full((CHUNK, 1), sink_ref[(hk * groups_per_kv + m) * 2 + half], F32))
        s = _dot_nt(jnp.concatenate(blocks, axis=0), keys[hk]) * ATTN_SCALE
        if valid is not None:
            s = jnp.where(valid, s, NEG_INF)
        sk = jnp.concatenate(sinks, axis=0)
        m_row = jnp.maximum(jnp.max(s, axis=-1, keepdims=True), sk)
        p = jnp.exp(s - m_row).astype(BF16)
        pv = jnp.dot(p, jnp.concatenate([vals[hk].astype(BF16), ones], axis=1), preferred_element_type=F32)
        o = pv[:, :LANES] / (pv[:, LANES:] + jnp.exp(sk - m_row))
        for m in range(groups_per_kv):
            out.append(jnp.where(in_a, o[2 * m * CHUNK:(2 * m + 1) * CHUNK],
                                 o[(2 * m + 1) * CHUNK:(2 * m + 2) * CHUNK]))
    return jnp.concatenate(out, axis=1)


def _attn_ffn_kernel(sink_ref, q_ref, kc_ref, vc_ref, kp_ref, vp_ref, x_ref, wo_ref, bo_ref, gain_ref,
                     win_ref, wout_ref, gfin_ref, o_ref, *, streaming, ff_chunk):
    tm = x_ref.shape[-2]
    n_chunks = tm // CHUNK
    kvw = kc_ref.shape[-1]
    n_kv = kvw // HEAD
    n_keys = WINDOW + CHUNK

    def dup(arr):
        in_a = lax.broadcasted_iota(jnp.int32, (arr.shape[0], LANES), 1) < HEAD
        res = []
        for j in range(kvw // LANES):
            res += list(_dup_heads(arr[:, j * LANES:(j + 1) * LANES], in_a))
        return res

    if streaming:
        q = q_ref[0]
        k_all = dup(jnp.concatenate([kp_ref[0], kc_ref[0]], axis=0))
        v_all = dup(jnp.concatenate([vp_ref[0], vc_ref[0]], axis=0))
        first_tile = pl.program_id(1) == 0
        key_idx = lax.broadcasted_iota(jnp.int32, (1, n_keys), 1)
    else:
        q = q_ref[...]

    outs = []
    for c in range(n_chunks):
        qc = q[c * CHUNK:(c + 1) * CHUNK]
        valid = None
        if streaming:
            keys = [ka[c * CHUNK:c * CHUNK + n_keys] for ka in k_all]
            vals = [va[c * CHUNK:c * CHUNK + n_keys] for va in v_all]
            if c * CHUNK < WINDOW:
                valid = jnp.logical_or(key_idx + c * CHUNK >= WINDOW, jnp.logical_not(first_tile))
        else:
            keys = dup(jnp.concatenate([kp_ref[c], kc_ref[c * CHUNK:(c + 1) * CHUNK]], axis=0))
            vals = dup(jnp.concatenate([vp_ref[c], vc_ref[c * CHUNK:(c + 1) * CHUNK]], axis=0))
        outs.append(_attend_chunk(qc, keys, vals, sink_ref, valid))
    o = jnp.concatenate(outs, axis=0)

    x = x_ref[0] if streaming else x_ref[...]
    x1 = x + _dot(o, wo_ref[...]) + bo_ref[...]
    x2 = _swiglu_residual(x1, gain_ref[...], win_ref, wout_ref, ff_chunk)
    y = _rms(x2) * gfin_ref[...]
    if streaming:
        o_ref[0] = y
    else:
        o_ref[...] = y


def _attn_ffn(q, k_new, v_new, k_prev, v_prev, x, sinks, w_o, b_o, gain, w_in, w_out, gain_fin, tm, ff_chunk):
    d = x.shape[-1]
    kvw = k_new.shape[-1]
    streaming = k_prev is None
    smem = pl.BlockSpec(memory_space=pltpu.SMEM)
    weights = (w_o, b_o.reshape(1, d), gain.reshape(1, d), w_in, w_out, gain_fin.reshape(1, d))
    if streaming:
        bsz, t_len, _ = x.shape
        per_win = tm // WINDOW
        grid = (bsz, t_len // tm)
        tile = lambda w: pl.BlockSpec((1, tm, w), lambda b, t: (b, t, 0))
        prev = pl.BlockSpec((1, WINDOW, kvw), lambda b, t: (b, jnp.maximum(t * per_win - 1, 0), 0))
        const = lambda arr: _resident(arr.shape, lambda b, t: (0, 0))
        in_specs = [smem, tile(d), tile(kvw), tile(kvw), prev, prev, tile(d)] + [const(a) for a in weights]
        args = (sinks, q, k_new, v_new, k_new, v_new, x) + weights
        out_spec = tile(d)
        sem = ("arbitrary", "arbitrary")
    else:
        n = x.shape[0]
        grid = (1,)
        full = lambda arr: pl.BlockSpec(arr.shape, lambda i: (0,) * arr.ndim)
        args = (sinks, q, k_new, v_new, k_prev, v_prev, x) + weights
        in_specs = [smem] + [full(a) for a in args[1:]]
        out_spec = pl.BlockSpec((n, d), lambda i: (0, 0))
        sem = ("arbitrary",)
    return pl.pallas_call(
        functools.partial(_attn_ffn_kernel, streaming=streaming, ff_chunk=ff_chunk),
        grid=grid,
        in_specs=in_specs,
        out_specs=out_spec,
        out_shape=jax.ShapeDtypeStruct(x.shape, F32),
        compiler_params=pltpu.CompilerParams(dimension_semantics=sem, vmem_limit_bytes=VMEM_LIMIT_BYTES),
    )(*args)


def _tile(n, cap):
    t = min(n, cap)
    while n % t:
        t -= 1
    return t


def kernel(x_prompt, x_sample, state_wkv, state_shift, cache_k, cache_v, norm_mix, norm_ffn, rw_mu, rw_w_rkv, rw_w0, rw_w1, rw_w2, rw_a0, rw_a1, rw_a2, rw_g1, rw_g2, rw_k_k, rw_k_a, rw_r_k, rw_lnx_w, rw_lnx_b, rw_w_o, kv_norm, w_kv, b_kv, w_q, b_q, attn_sinks, w_o, b_o, ffn_w_in, ffn_w_out, norm_final):
    depth, d = norm_mix.shape
    assert depth == 2 and rw_mu.shape[0] == 1 and w_q.shape[0] == 1, "one RWKV layer followed by one attention layer"
    n_heads = d // HEAD
    n_ff = ffn_w_out.shape[1]
    ff_chunk = 256 if n_ff % 256 == 0 else n_ff
    bf = lambda w: w.astype(BF16)
    wrkv, w1, w2, a1, a2, g1, g2 = map(bf, (rw_w_rkv[0], rw_w1[0], rw_w2[0], rw_a1[0], rw_a2[0], rw_g1[0], rw_g2[0]))
    rw_wo, wq, wkv, wo = bf(rw_w_o[0]), bf(w_q[0]), bf(w_kv), bf(w_o[0])
    win, wout = bf(ffn_w_in), bf(ffn_w_out)
    sinks = attn_sinks[0].reshape(-1)

    def run(x, pos_base, shift0, st0, win_k, win_v):
        bsz, t_len, _ = x.shape
        n_rows = bsz * t_len
        r, w, k, v, a, b, g, shift = _time_mix_proj(
            x, shift0, norm_mix[0], rw_mu[0], wrkv, rw_w0[0], w1, w2, rw_a0[0], a1, a2, g1, g2,
            rw_k_k[0], rw_k_a[0], tm=_tile(t_len, 256))
        z, st = _wkv(r, w, k, v, a, b, st0, rw_r_k[0].reshape(-1), rw_lnx_w[0], rw_lnx_b[0],
                     n_sub=_tile(t_len // CHUNK, 8), n_pp=2)
        flat = lambda t: t.reshape(n_rows, t.shape[-1])
        x1 = _gate_ffn(flat(z), flat(g), flat(x), rw_wo, norm_ffn[0], win[0], wout[0],
                       tm=_tile(n_rows, 512), ff_chunk=ff_chunk)
        q, k_new, v_new = _qkv_proj(x1, norm_mix[1], kv_norm, wq, b_q[0], wkv, b_kv, t_len, pos_base,
                                    tm=_tile(n_rows, 512))
        kvw = k_new.shape[-1]
        k_new = k_new.reshape(bsz, t_len, kvw)
        v_new = v_new.reshape(bsz, t_len, kvw)
        if win_k is None:
            y = _attn_ffn(q.reshape(bsz, t_len, d), k_new, v_new, None, None, x1.reshape(bsz, t_len, d), sinks,
                          wo, b_o[0], norm_ffn[1], win[1], wout[1], norm_final, tm=_tile(t_len, 512),
                          ff_chunk=ff_chunk)
            k_state, v_state = k_new[:, -WINDOW:], v_new[:, -WINDOW:]
        else:
            assert t_len == CHUNK
            kp = win_k.reshape(bsz, WINDOW, kvw)
            vp = win_v.reshape(bsz, WINDOW, kvw)
            y = _attn_ffn(q, k_new.reshape(n_rows, kvw), v_new.reshape(n_rows, kvw), kp, vp, x1, sinks,
                          wo, b_o[0], norm_ffn[1], win[1], wout[1], norm_final, tm=n_rows, ff_chunk=ff_chunk)
            k_state = jnp.concatenate([kp, k_new], axis=1)[:, -WINDOW:]
            v_state = jnp.concatenate([vp, v_new], axis=1)[:, -WINDOW:]
        n_kv = kvw // HEAD
        st = st.reshape(bsz, HEAD, n_heads, HEAD).transpose(0, 2, 3, 1)[None]
        return (y.reshape(bsz, t_len, d), st, shift.reshape(1, bsz, d),
                k_state.reshape(bsz, WINDOW, n_kv, HEAD), v_state.reshape(bsz, WINDOW, n_kv, HEAD))

    bp = x_prompt.shape[0]
    zeros_state = jnp.zeros((bp, HEAD, d), F32)
    y_p, wkv_p, shift_p, k_p, v_p = run(x_prompt, 0, jnp.zeros((bp, d), F32), zeros_state, None, None)
    st_s = state_wkv[0].transpose(0, 3, 1, 2).reshape(x_sample.shape[0], HEAD, d)
    y_s, wkv_s, shift_s, k_s, v_s = run(x_sample, PAST_LEN, state_shift[0], st_s, cache_k, cache_v)
    return (y_p, y_s, wkv_p, shift_p, k_p, v_p, wkv_s, shift_s, k_s, v_s)
```

```python
import functools

import jax
import jax.numpy as jnp
from jax import lax
from jax.experimental import pallas as pl
from jax.experimental.pallas import tpu as pltpu

F32 = jnp.float32
BF16 = jnp.bfloat16

HEAD = 64
LANES = 128
CHUNK = 64
WINDOW = 128
ROT_DIM = HEAD // 4
ROPE_THETA = 500000.0
RMS_EPS = 1e-5
LNX_EPS = 64e-5
NEG_INF = -1e30
ATTN_SCALE = 1.0 / (HEAD ** 0.5)
VMEM_LIMIT_BYTES = 56 * 1024 * 1024
PAST_LEN = 1024


def _resident(shape, index_map):
    return pl.BlockSpec(shape, index_map, pipeline_mode=pl.Buffered(1))


def _dot(a, b):
    return jnp.dot(a.astype(BF16), b.astype(BF16), preferred_element_type=F32)


def _dot_nt(a, b):
    return lax.dot_general(a.astype(BF16), b.astype(BF16), (((1,), (1,)), ((), ())),
                           preferred_element_type=F32)


def _dot_tn(a, b):
    return lax.dot_general(a.astype(BF16), b.astype(BF16), (((0,), (0,)), ((), ())),
                           preferred_element_type=F32)


def _split2(x):
    hi = x.astype(BF16)
    return hi, (x - hi.astype(F32)).astype(BF16)


def _rms(x):
    return x * lax.rsqrt(jnp.mean(x * x, axis=-1, keepdims=True) + RMS_EPS)


def _sigmoid(x):
    return 1.0 / (1.0 + jnp.exp(-x))


def _head_ones2():
    r = lax.broadcasted_iota(jnp.int32, (2 * LANES, LANES), 0)
    c = lax.broadcasted_iota(jnp.int32, (2 * LANES, LANES), 1)
    return jnp.where(((r % LANES) // HEAD) == (c // HEAD), 1.0, 0.0).astype(BF16)


def _head_sum(x, ones2):
    hi, lo = _split2(x)
    return jnp.dot(jnp.concatenate([hi, lo], axis=1), ones2, preferred_element_type=F32)


def _head_sums(xs, ones2):
    lhs = jnp.concatenate([jnp.concatenate(list(_split2(x)), axis=1) for x in xs], axis=0)
    out = jnp.dot(lhs, ones2, preferred_element_type=F32)
    return [out[i * CHUNK:(i + 1) * CHUNK] for i in range(len(xs))]


def _pair_rhs(q, in_a):
    zero = jnp.zeros_like(q)
    return jnp.concatenate([jnp.where(in_a, q, zero), jnp.where(in_a, zero, q)], axis=0).astype(BF16)


def _pairmm(p, q, in_a):
    return jnp.dot(p.astype(BF16), _pair_rhs(q, in_a), preferred_element_type=F32)


def _mix_kernel(x_ref, sh0_ref, gain_ref, mu_ref, wrkv_ref, w0_ref, w1_ref, w2_ref, a0_ref, a1_ref, a2_ref,
                g1_ref, g2_ref, kk_ref, ka_ref,
                r_out, w_out, k_out, v_out, a_out, b_out, g_out, sh_out, carry_ref):
    t = pl.program_id(1)

    @pl.when(t == 0)
    def _():
        carry_ref[...] = sh0_ref[0]

    x = x_ref[0]
    tm, d = x.shape
    h = _rms(x) * gain_ref[...]
    rows = lax.broadcasted_iota(jnp.int32, (tm, 1), 0)
    prev = jnp.where(rows == 0, carry_ref[...], pltpu.roll(h, 1, 0))
    last = h[tm - 1:tm, :]
    carry_ref[...] = last
    sh_out[0] = last

    xx = prev - h
    xr = h + xx * mu_ref[0:1, :]
    xw = h + xx * mu_ref[1:2, :]
    xk = h + xx * mu_ref[2:3, :]
    xv = h + xx * mu_ref[3:4, :]
    xa = h + xx * mu_ref[4:5, :]
    xg = h + xx * mu_ref[5:6, :]

    r = _dot(xr, wrkv_ref[0])
    k = _dot(xk, wrkv_ref[1])
    v = _dot(xv, wrkv_ref[2])
    wl = w0_ref[...] + _dot(jnp.tanh(_dot(xw, w1_ref[...])), w2_ref[...])
    sp = jnp.maximum(-wl, 0.0) + jnp.log(1.0 + jnp.exp(-jnp.abs(wl)))
    logdecay = -jnp.exp(-sp - 0.5)
    a = _sigmoid(a0_ref[...] + _dot(_dot(xa, a1_ref[...]), a2_ref[...]))
    g = _dot(_sigmoid(_dot(xg, g1_ref[...])), g2_ref[...])

    kk = k * kk_ref[...]
    ones2 = _head_ones2()
    kkn = []
    for p in range(d // LANES):
        kp = kk[:, p * LANES:(p + 1) * LANES]
        ss = _head_sum(kp * kp, ones2)
        kkn.append(kp * lax.rsqrt(jnp.maximum(ss, 1e-24)))
    kkn = jnp.concatenate(kkn, axis=1)

    r_out[0] = r
    w_out[0] = logdecay
    k_out[0] = k * (1.0 + (a - 1.0) * ka_ref[...])
    v_out[0] = v
    a_out[0] = -kkn
    b_out[0] = kkn * a
    g_out[0] = g


def _time_mix_proj(x, shift0, gain, mu, wrkv, w0, w1, w2, a0, a1, a2, g1, g2, k_k, k_a, tm):
    bsz, t_len, d = x.shape
    n_t = t_len // tm
    row = lambda arr: arr.reshape(1, -1)
    const2 = lambda shape: _resident(shape, lambda b, t: (0, 0))
    const3 = lambda shape: _resident(shape, lambda b, t: (0, 0, 0))
    tile = pl.BlockSpec((1, tm, d), lambda b, t: (b, t, 0))
    per_b = pl.BlockSpec((1, 1, d), lambda b, t: (b, 0, 0))
    act = jax.ShapeDtypeStruct((bsz, t_len, d), F32)
    outs = pl.pallas_call(
        _mix_kernel,
        grid=(bsz, n_t),
        in_specs=[tile, per_b, const2((1, d)), const2(mu.shape), const3(wrkv.shape), const2((1, d)),
                  const2(w1.shape), const2(w2.shape), const2((1, d)), const2(a1.shape), const2(a2.shape),
                  const2(g1.shape), const2(g2.shape), const2((1, d)), const2((1, d))],
        out_specs=[tile] * 7 + [per_b],
        out_shape=[act] * 7 + [jax.ShapeDtypeStruct((bsz, 1, d), F32)],
        scratch_shapes=[pltpu.VMEM((1, d), F32)],
        compiler_params=pltpu.CompilerParams(dimension_semantics=("arbitrary", "arbitrary"),
                                             vmem_limit_bytes=VMEM_LIMIT_BYTES),
    )(x, shift0.reshape(bsz, 1, d), row(gain), mu, wrkv, row(w0), w1, w2, row(a0), a1, a2, g1, g2,
      row(k_k), row(k_a))
    return outs


def _wkv_kernel(r_ref, w_ref, k_ref, v_ref, a_ref, b_ref, s0_ref, rk_ref, lw_ref, lb_ref,
                z_ref, sout_ref, st_ref, *, n_sub):
    c = pl.program_id(2)

    @pl.when(c == 0)
    def _():
        st_ref[...] = s0_ref[0]

    shape = (CHUNK, LANES)
    row = lax.broadcasted_iota(jnp.int32, shape, 0)
    lane = lax.broadcasted_iota(jnp.int32, shape, 1)
    in_a = lane < HEAD
    li = lane % HEAD
    strict = li < row
    incl = li <= row
    eye2 = jnp.where(li == row, 1.0, 0.0).astype(F32)
    tri2 = jnp.where(incl, 1.0, 0.0).astype(BF16)
    ones2 = _head_ones2()
    n_pp = st_ref.shape[1] // LANES
    items = [(s, p) for s in range(n_sub) for p in range(n_pp)]

    def ld(ref):
        return [ref[0, s * CHUNK:(s + 1) * CHUNK, p * LANES:(p + 1) * LANES] for s, p in items]

    def each(fn, *cols):
        return [fn(*xs) for xs in zip(*cols)]

    r, w, k, v, a, b = ld(r_ref), ld(w_ref), ld(k_ref), ld(v_ref), ld(a_ref), ld(b_ref)

    g = each(lambda x: jnp.dot(tri2, jnp.concatenate(list(_split2(x)), axis=0), preferred_element_type=F32), w)
    e = each(jnp.exp, g)
    ei = each(lambda x: jnp.exp(-x), g)
    at = each(lambda a_, e_, w_: a_ * e_ * jnp.exp(-w_), a, e, w)
    rt = each(lambda r_, e_: r_ * e_, r, e)
    bt = each(lambda b_, x: b_ * x, b, ei)
    kt = each(lambda k_, x: k_ * x, k, ei)
    e_last = each(lambda e_: e_[CHUNK - 1:CHUNK, :], e)
    bdec = each(lambda x, el: x * el, bt, e_last)
    kdec = each(lambda x, el: x * el, kt, e_last)

    def pair_products(at_, rt_, bt_, kt_):
        zero = jnp.zeros_like(bt_)
        rhs4 = jnp.concatenate([jnp.where(in_a, bt_, zero), jnp.where(in_a, zero, bt_),
                                jnp.where(in_a, kt_, zero), jnp.where(in_a, zero, kt_)], axis=0)
        return _dot_nt(jnp.concatenate([at_, rt_], axis=0), rhs4)

    aa = each(pair_products, at, rt, bt, kt)
    a_ab = each(lambda x: jnp.where(strict, x[0:CHUNK, 0:LANES], 0.0), aa)
    a_ak = each(lambda x: jnp.where(strict, x[0:CHUNK, LANES:2 * LANES], 0.0), aa)
    a_rb = each(lambda x: jnp.where(incl, x[CHUNK:2 * CHUNK, 0:LANES], 0.0), aa)
    a_rk = each(lambda x: jnp.where(incl, x[CHUNK:2 * CHUNK, LANES:2 * LANES], 0.0), aa)

    mm = lambda p_, q_: _pairmm(p_, q_, in_a)
    def square_and_extend(xp, ti):
        res = jnp.dot(jnp.concatenate([xp, ti], axis=0).astype(BF16), _pair_rhs(xp, in_a),
                      preferred_element_type=F32)
        return res[0:CHUNK], ti + res[CHUNK:2 * CHUNK]

    xpow = each(mm, a_ab, a_ab)
    tinv = each(lambda x: eye2 + x, a_ab)
    for _ in range(4):
        both = each(square_and_extend, xpow, tinv)
        xpow = [x for x, _ in both]
        tinv = [t_ for _, t_ in both]
    tinv = each(lambda t_, x: t_ + mm(t_, x), tinv, xpow)

    at2 = each(mm, tinv, at)
    wv = each(mm, tinv, each(mm, a_ak, v))
    pn = each(lambda bd, kd, at2_, wv_, v_: _dot_tn(
        jnp.concatenate([bd, kd], axis=0),
        jnp.concatenate([jnp.concatenate([at2_, wv_], axis=1),
                         jnp.concatenate([jnp.zeros_like(v_), v_], axis=1)], axis=0)), bdec, kdec, at2, wv, v)
    pick = lambda x: jnp.where(in_a, x[0:CHUNK], x[CHUNK:2 * CHUNK])
    p_mat = each(lambda x: pick(x[:, 0:LANES]), pn)
    n_mat = each(lambda x: pick(x[:, LANES:2 * LANES]), pn)
    q_mat = each(lambda rt_, arb, at2_: rt_ + mm(arb, at2_), rt, a_rb, at2)
    y0 = each(lambda arb, ark, wv_, v_: jnp.dot(
        jnp.concatenate([arb, ark], axis=1).astype(BF16),
        jnp.concatenate([_pair_rhs(wv_, in_a), _pair_rhs(v_, in_a)], axis=0), preferred_element_type=F32),
        a_rb, a_rk, wv, v)
    rk = [rk_ref[0, :, p * LANES:(p + 1) * LANES] for _, p in items]
    sums = _head_sums(each(lambda el: eye2 * el, e_last) + each(lambda r_, k_, rk_: r_ * k_ * rk_, r, k, rk),
                      ones2)
    decay, bonus = sums[:len(items)], sums[len(items):]

    st = [st_ref[:, p * LANES:(p + 1) * LANES] for p in range(n_pp)]
    y = []
    for i, (s, p) in enumerate(items):
        res = jnp.dot(jnp.concatenate([q_mat[i], p_mat[i]], axis=0).astype(BF16), _pair_rhs(st[p], in_a),
                      preferred_element_type=F32)
        y.append(res[0:CHUNK] + y0[i])
        st[p] = decay[i] * st[p] + res[CHUNK:2 * CHUNK] + n_mat[i]

    yc = each(lambda y_, m_: y_ - m_ * (1.0 / HEAD), y, _head_sums(y, ones2))
    var = each(lambda x: x * (1.0 / HEAD), _head_sums(each(lambda x: x * x, yc), ones2))
    for i, (s, p) in enumerate(items):
        lanes = slice(p * LANES, (p + 1) * LANES)
        yn = yc[i] * lax.rsqrt(var[i] + LNX_EPS) * lw_ref[0, :, lanes] + lb_ref[0, :, lanes]
        z_ref[0, s * CHUNK:(s + 1) * CHUNK, lanes] = yn + bonus[i] * v[i]

    for p in range(n_pp):
        st_ref[:, p * LANES:(p + 1) * LANES] = st[p]

    @pl.when(c == pl.num_programs(2) - 1)
    def _():
        for p in range(n_pp):
            sout_ref[0, :, p * LANES:(p + 1) * LANES] = st[p]


def _wkv(r, w, k, v, a, b, st0, r_k, lnx_w, lnx_b, n_sub, n_pp):
    bsz, t_len, d = r.shape
    width = n_pp * LANES
    rows = n_sub * CHUNK
    seq = pl.BlockSpec((1, rows, width), lambda bi, p, c: (bi, c, p))
    state = pl.BlockSpec((1, HEAD, width), lambda bi, p, c: (bi, 0, p))
    vec = pl.BlockSpec((1, 1, width), lambda bi, p, c: (p, 0, 0))
    per_group = lambda arr: arr.reshape(d // width, 1, width)
    return pl.pallas_call(
        functools.partial(_wkv_kernel, n_sub=n_sub),
        grid=(bsz, d // width, t_len // rows),
        in_specs=[seq] * 6 + [state, vec, vec, vec],
        out_specs=[seq, state],
        out_shape=[jax.ShapeDtypeStruct((bsz, t_len, d), F32), jax.ShapeDtypeStruct((bsz, HEAD, d), F32)],
        scratch_shapes=[pltpu.VMEM((HEAD, width), F32)],
        compiler_params=pltpu.CompilerParams(dimension_semantics=("arbitrary", "arbitrary", "arbitrary"),
                                             vmem_limit_bytes=VMEM_LIMIT_BYTES),
    )(r, w, k, v, a, b, st0, per_group(r_k), per_group(lnx_w), per_group(lnx_b))


def _swiglu_residual(x, gain, win_ref, wout_ref, ff_chunk):
    n_ff = wout_ref.shape[0]
    hn = (_rms(x) * gain).astype(BF16)
    acc = x
    for j in range(n_ff // ff_chunk):
        lo = j * ff_chunk
        gate = jnp.dot(hn, win_ref[:, lo:lo + ff_chunk], preferred_element_type=F32)
        up = jnp.dot(hn, win_ref[:, n_ff + lo:n_ff + lo + ff_chunk], preferred_element_type=F32)
        hid = gate * _sigmoid(gate) * up
        acc = acc + jnp.dot(hid.astype(BF16), wout_ref[lo:lo + ff_chunk, :], preferred_element_type=F32)
    return acc


def _gate_ffn_kernel(z_ref, g_ref, x_ref, wo_ref, gain_ref, win_ref, wout_ref, o_ref, *, ff_chunk):
    x1 = x_ref[...] + _dot(z_ref[...] * g_ref[...], wo_ref[...])
    o_ref[...] = _swiglu_residual(x1, gain_ref[...], win_ref, wout_ref, ff_chunk)


def _gate_ffn(z, g, x, w_o, gain, w_in, w_out, tm, ff_chunk):
    n, d = x.shape
    tile = pl.BlockSpec((tm, d), lambda i: (i, 0))
    const = lambda arr: _resident(arr.shape, lambda i: (0, 0))
    gain = gain.reshape(1, d)
    return pl.pallas_call(
        functools.partial(_gate_ffn_kernel, ff_chunk=ff_chunk),
        grid=(n // tm,),
        in_specs=[tile, tile, tile, const(w_o), const(gain), const(w_in), const(w_out)],
        out_specs=tile,
        out_shape=jax.ShapeDtypeStruct((n, d), F32),
        compiler_params=pltpu.CompilerParams(dimension_semantics=("arbitrary",),
                                             vmem_limit_bytes=VMEM_LIMIT_BYTES),
    )(z, g, x, w_o, gain, w_in, w_out)


def _rope(x, cs, sn, first):
    out = []
    for p in range(x.shape[1] // LANES):
        xp = x[:, p * LANES:(p + 1) * LANES]
        partner = jnp.where(first, pltpu.roll(xp, LANES - ROT_DIM // 2, 1), pltpu.roll(xp, ROT_DIM // 2, 1))
        out.append(xp * cs + partner * sn)
    return jnp.concatenate(out, axis=1)


def _qkv_kernel(x_ref, gq_ref, gkv_ref, wq_ref, bq_ref, wkv_ref, bkv_ref, inv_ref, q_out, k_out, v_out,
                *, seq_len, pos_base):
    x = x_ref[...]
    tm = x.shape[0]
    kvw = k_out.shape[1]
    y = _rms(x)
    q = _dot(y * gq_ref[...], wq_ref[...]) + bq_ref[...]
    kv = _dot(y * gkv_ref[...], wkv_ref[...]) + bkv_ref[...]

    rows = lax.broadcasted_iota(jnp.int32, (tm, 1), 0) + pl.program_id(0) * tm
    pos = (lax.rem(rows, seq_len) + pos_base).astype(F32)
    ang = pos * inv_ref[...]
    lane = lax.broadcasted_iota(jnp.int32, (1, LANES), 1) % HEAD
    first = lane < ROT_DIM // 2
    cs = jnp.cos(ang)
    sn = jnp.where(first, -1.0, 1.0) * jnp.sin(ang)
    q_out[...] = _rope(q, cs, sn, first)
    k_out[...] = _rope(kv[:, :kvw], cs, sn, first)
    v_out[...] = kv[:, kvw:]


def _qkv_proj(x, gain_q, gain_kv, w_q, b_q, w_kv, b_kv, seq_len, pos_base, tm):
    n, d = x.shape
    kvw = w_kv.shape[1] // 2
    half = ROT_DIM // 2
    lane = jnp.arange(LANES) % HEAD
    inv = jnp.power(jnp.float32(ROPE_THETA), -(lane % half).astype(F32) * (2.0 / ROT_DIM))
    inv = jnp.where(lane < ROT_DIM, inv, 0.0).reshape(1, LANES)
    tile = lambda w: pl.BlockSpec((tm, w), lambda i: (i, 0))
    const = lambda arr: _resident(arr.shape, lambda i: (0, 0))
    args = (x, gain_q.reshape(1, d), gain_kv.reshape(1, d), w_q, b_q.reshape(1, -1), w_kv,
            b_kv.reshape(1, -1), inv)
    return pl.pallas_call(
        functools.partial(_qkv_kernel, seq_len=seq_len, pos_base=pos_base),
        grid=(n // tm,),
        in_specs=[tile(d)] + [const(a) for a in args[1:]],
        out_specs=[tile(w_q.shape[1]), tile(kvw), tile(kvw)],
        out_shape=[jax.ShapeDtypeStruct((n, w_q.shape[1]), F32), jax.ShapeDtypeStruct((n, kvw), F32),
                   jax.ShapeDtypeStruct((n, kvw), F32)],
        compiler_params=pltpu.CompilerParams(dimension_semantics=("arbitrary",),
                                             vmem_limit_bytes=VMEM_LIMIT_BYTES),
    )(*args)


def _dup_heads(kp, in_a):
    rolled = pltpu.roll(kp, HEAD, 1)
    return jnp.where(in_a, kp, rolled), jnp.where(in_a, rolled, kp)


def _attend_chunk(q, keys, vals, sink_ref, valid):
    n_kv = len(keys)
    groups_per_kv = q.shape[1] // LANES // n_kv
    in_a = lax.broadcasted_iota(jnp.int32, (CHUNK, LANES), 1) < HEAD
    n_keys = keys[0].shape[0]
    ones = jnp.ones((n_keys, LANES), BF16)
    out = []
    for hk in range(n_kv):
        blocks, sinks = [], []
        for m in range(groups_per_kv):
            qg = q[:, (hk * groups_per_kv + m) * LANES:(hk * groups_per_kv + m + 1) * LANES] * ATTN_SCALE
            zero = jnp.zeros_like(qg)
            blocks += [jnp.where(in_a, qg, zero), jnp.where(in_a, zero, qg)]
            for half in range(2):
                sinks.append(jnp.full((CHUNK, 1), sink_ref[(hk * groups_per_kv + m) * 2 + half], F32))
        s = _dot_nt(jnp.concatenate(blocks, axis=0), keys[hk])
        if valid is not None:
            s = jnp.where(valid, s, NEG_INF)
        sk = jnp.concatenate(sinks, axis=0)
        m_row = jnp.maximum(jnp.max(s, axis=-1, keepdims=True), sk)
        p = jnp.exp(s - m_row).astype(BF16)
        pv = jnp.dot(p, jnp.concatenate([vals[hk].astype(BF16), ones], axis=1), preferred_element_type=F32)
        o = pv[:, :LANES] / (pv[:, LANES:] + jnp.exp(sk - m_row))
        for m in range(groups_per_kv):
            out.append(jnp.where(in_a, o[2 * m * CHUNK:(2 * m + 1) * CHUNK],
                                 o[(2 * m + 1) * CHUNK:(2 * m + 2) * CHUNK]))
    return jnp.concatenate(out, axis=1)


def _attn_ffn_kernel(sink_ref, q_ref, kc_ref, vc_ref, kp_ref, vp_ref, x_ref, wo_ref, bo_ref, gain_ref,
                     win_ref, wout_ref, gfin_ref, o_ref, *, streaming, ff_chunk):
    tm = x_ref.shape[-2]
    n_chunks = tm // CHUNK
    kvw = kc_ref.shape[-1]
    n_kv = kvw // HEAD
    n_keys = WINDOW + CHUNK

    def dup(arr):
        in_a = lax.broadcasted_iota(jnp.int32, (arr.shape[0], LANES), 1) < HEAD
        res = []
        for j in range(kvw // LANES):
            res += list(_dup_heads(arr[:, j * LANES:(j + 1) * LANES], in_a))
        return res

    if streaming:
        q = q_ref[0]
        k_all = dup(jnp.concatenate([kp_ref[0], kc_ref[0]], axis=0))
        v_all = dup(jnp.concatenate([vp_ref[0], vc_ref[0]], axis=0))
        first_tile = pl.program_id(1) == 0
        key_idx = lax.broadcasted_iota(jnp.int32, (1, n_keys), 1)
    else:
        q = q_ref[...]

    outs = []
    for c in range(n_chunks):
        qc = q[c * CHUNK:(c + 1) * CHUNK]
        valid = None
        if streaming:
            keys = [ka[c * CHUNK:c * CHUNK + n_keys] for ka in k_all]
            vals = [va[c * CHUNK:c * CHUNK + n_keys] for va in v_all]
            if c * CHUNK < WINDOW:
                valid = jnp.logical_or(key_idx + c * CHUNK >= WINDOW, jnp.logical_not(first_tile))
        else:
            keys = dup(jnp.concatenate([kp_ref[c], kc_ref[c * CHUNK:(c + 1) * CHUNK]], axis=0))
            vals = dup(jnp.concatenate([vp_ref[c], vc_ref[c * CHUNK:(c + 1) * CHUNK]], axis=0))
        outs.append(_attend_chunk(qc, keys, vals, sink_ref, valid))
    o = jnp.concatenate(outs, axis=0)

    x = x_ref[0] if streaming else x_ref[...]
    x1 = x + _dot(o, wo_ref[...]) + bo_ref[...]
    x2 = _swiglu_residual(x1, gain_ref[...], win_ref, wout_ref, ff_chunk)
    y = _rms(x2) * gfin_ref[...]
    if streaming:
        o_ref[0] = y
    else:
        o_ref[...] = y


def _attn_ffn(q, k_new, v_new, k_prev, v_prev, x, sinks, w_o, b_o, gain, w_in, w_out, gain_fin, tm, ff_chunk):
    d = x.shape[-1]
    kvw = k_new.shape[-1]
    streaming = k_prev is None
    smem = pl.BlockSpec(memory_space=pltpu.SMEM)
    weights = (w_o, b_o.reshape(1, d), gain.reshape(1, d), w_in, w_out, gain_fin.reshape(1, d))
    if streaming:
        bsz, t_len, _ = x.shape
        per_win = tm // WINDOW
        grid = (bsz, t_len // tm)
        tile = lambda w: pl.BlockSpec((1, tm, w), lambda b, t: (b, t, 0))
        prev = pl.BlockSpec((1, WINDOW, kvw), lambda b, t: (b, jnp.maximum(t * per_win - 1, 0), 0))
        const = lambda arr: _resident(arr.shape, lambda b, t: (0, 0))
        in_specs = [smem, tile(d), tile(kvw), tile(kvw), prev, prev, tile(d)] + [const(a) for a in weights]
        args = (sinks, q, k_new, v_new, k_new, v_new, x) + weights
        out_spec = tile(d)
        sem = ("arbitrary", "arbitrary")
    else:
        n = x.shape[0]
        grid = (1,)
        full = lambda arr: pl.BlockSpec(arr.shape, lambda i: (0,) * arr.ndim)
        args = (sinks, q, k_new, v_new, k_prev, v_prev, x) + weights
        in_specs = [smem] + [full(a) for a in args[1:]]
        out_spec = pl.BlockSpec((n, d), lambda i: (0, 0))
        sem = ("arbitrary",)
    return pl.pallas_call(
        functools.partial(_attn_ffn_kernel, streaming=streaming, ff_chunk=ff_chunk),
        grid=grid,
        in_specs=in_specs,
        out_specs=out_spec,
        out_shape=jax.ShapeDtypeStruct(x.shape, F32),
        compiler_params=pltpu.CompilerParams(dimension_semantics=sem, vmem_limit_bytes=VMEM_LIMIT_BYTES),
    )(*args)


def _tile(n, cap):
    t = min(n, cap)
    while n % t:
        t -= 1
    return t


def kernel(x_prompt, x_sample, state_wkv, state_shift, cache_k, cache_v, norm_mix, norm_ffn, rw_mu, rw_w_rkv, rw_w0, rw_w1, rw_w2, rw_a0, rw_a1, rw_a2, rw_g1, rw_g2, rw_k_k, rw_k_a, rw_r_k, rw_lnx_w, rw_lnx_b, rw_w_o, kv_norm, w_kv, b_kv, w_q, b_q, attn_sinks, w_o, b_o, ffn_w_in, ffn_w_out, norm_final):
    depth, d = norm_mix.shape
    assert depth == 2 and rw_mu.shape[0] == 1 and w_q.shape[0] == 1, "one RWKV layer followed by one attention layer"
    n_heads = d // HEAD
    n_ff = ffn_w_out.shape[1]
    ff_chunk = 256 if n_ff % 256 == 0 else n_ff
    bf = lambda w: w.astype(BF16)
    wrkv, w1, w2, a1, a2, g1, g2 = map(bf, (rw_w_rkv[0], rw_w1[0], rw_w2[0], rw_a1[0], rw_a2[0], rw_g1[0], rw_g2[0]))
    rw_wo, wq, wkv, wo = bf(rw_w_o[0]), bf(w_q[0]), bf(w_kv), bf(w_o[0])
    win, wout = bf(ffn_w_in), bf(ffn_w_out)
    sinks = attn_sinks[0].reshape(-1)

    def run(x, pos_base, shift0, st0, win_k, win_v):
        bsz, t_len, _ = x.shape
        n_rows = bsz * t_len
        r, w, k, v, a, b, g, shift = _time_mix_proj(
            x, shift0, norm_mix[0], rw_mu[0], wrkv, rw_w0[0], w1, w2, rw_a0[0], a1, a2, g1, g2,
            rw_k_k[0], rw_k_a[0], tm=_tile(t_len, 256))
        z, st = _wkv(r, w, k, v, a, b, st0, rw_r_k[0].reshape(-1), rw_lnx_w[0], rw_lnx_b[0],
                     n_sub=_tile(t_len // CHUNK, 2), n_pp=d // LANES)
        flat = lambda t: t.reshape(n_rows, t.shape[-1])
        x1 = _gate_ffn(flat(z), flat(g), flat(x), rw_wo, norm_ffn[0], win[0], wout[0],
                       tm=_tile(n_rows, 512), ff_chunk=ff_chunk)
        q, k_new, v_new = _qkv_proj(x1, norm_mix[1], kv_norm, wq, b_q[0], wkv, b_kv, t_len, pos_base,
                                    tm=_tile(n_rows, 512))
        kvw = k_new.shape[-1]
        k_new = k_new.reshape(bsz, t_len, kvw)
        v_new = v_new.reshape(bsz, t_len, kvw)
        if win_k is None:
            y = _attn_ffn(q.reshape(bsz, t_len, d), k_new, v_new, None, None, x1.reshape(bsz, t_len, d), sinks,
                          wo, b_o[0], norm_ffn[1], win[1], wout[1], norm_final, tm=_tile(t_len, 512),
                          ff_chunk=ff_chunk)
            k_state, v_state = k_new[:, -WINDOW:], v_new[:, -WINDOW:]
        else:
            assert t_len == CHUNK
            kp = win_k.reshape(bsz, WINDOW, kvw)
            vp = win_v.reshape(bsz, WINDOW, kvw)
            y = _attn_ffn(q, k_new.reshape(n_rows, kvw), v_new.reshape(n_rows, kvw), kp, vp, x1, sinks,
                          wo, b_o[0], norm_ffn[1], win[1], wout[1], norm_final, tm=n_rows, ff_chunk=ff_chunk)
            k_state = jnp.concatenate([kp, k_new], axis=1)[:, -WINDOW:]
            v_state = jnp.concatenate([vp, v_new], axis=1)[:, -WINDOW:]
        n_kv = kvw // HEAD
        st = st.reshape(bsz, HEAD, n_heads, HEAD).transpose(0, 2, 3, 1)[None]
        return (y.reshape(bsz, t_len, d), st, shift.reshape(1, bsz, d),
                k_state.reshape(bsz, WINDOW, n_kv, HEAD), v_state.reshape(bsz, WINDOW, n_kv, HEAD))

    bp = x_prompt.shape[0]
    zeros_state = jnp.zeros((bp, HEAD, d), F32)
    y_p, wkv_p, shift_p, k_p, v_p = run(x_prompt, 0, jnp.zeros((bp, d), F32), zeros_state, None, None)
    st_s = state_wkv[0].transpose(0, 3, 1, 2).reshape(x_sample.shape[0], HEAD, d)
    y_s, wkv_s, shift_s, k_s, v_s = run(x_sample, PAST_LEN, state_shift[0], st_s, cache_k, cache_v)
    return (y_p, y_s, wkv_p, shift_p, k_p, v_p, wkv_s, shift_s, k_s, v_s)
```

```python
import functools

import jax
import jax.numpy as jnp
from jax import lax
from jax.experimental import pallas as pl
from jax.experimental.pallas import tpu as pltpu

F32 = jnp.float32
BF16 = jnp.bfloat16

HEAD = 64
LANES = 128
SUBLANES = 8
CHUNK = 64
WINDOW = 128
ROT_DIM = HEAD // 4
ROPE_THETA = 500000.0
RMS_EPS = 1e-5
LNX_EPS = 64e-5
NEG_INF = -1e30
ATTN_SCALE = 1.0 / (HEAD ** 0.5)
VMEM_LIMIT_BYTES = 56 * 1024 * 1024
PAST_LEN = 1024


def _resident(shape, index_map):
    return pl.BlockSpec(shape, index_map, pipeline_mode=pl.Buffered(1))


def _dot(a, b):
    return jnp.dot(a.astype(BF16), b.astype(BF16), preferred_element_type=F32)


def _dot_nt(a, b):
    return lax.dot_general(a.astype(BF16), b.astype(BF16), (((1,), (1,)), ((), ())),
                           preferred_element_type=F32)


def _dot_tn(a, b):
    return lax.dot_general(a.astype(BF16), b.astype(BF16), (((0,), (0,)), ((), ())),
                           preferred_element_type=F32)


def _split2(x):
    hi = x.astype(BF16)
    return hi, (x - hi.astype(F32)).astype(BF16)


def _rms(x):
    return x * lax.rsqrt(jnp.mean(x * x, axis=-1, keepdims=True) + RMS_EPS)


def _sigmoid(x):
    return 1.0 / (1.0 + jnp.exp(-x))


def _head_ones2():
    r = lax.broadcasted_iota(jnp.int32, (2 * LANES, LANES), 0)
    c = lax.broadcasted_iota(jnp.int32, (2 * LANES, LANES), 1)
    return jnp.where(((r % LANES) // HEAD) == (c // HEAD), 1.0, 0.0).astype(BF16)


def _head_sums(xs, ones2):
    lhs = jnp.concatenate([jnp.concatenate(list(_split2(x)), axis=1) for x in xs], axis=0)
    out = jnp.dot(lhs, ones2, preferred_element_type=F32)
    res, lo = [], 0
    for x in xs:
        res.append(out[lo:lo + x.shape[0]])
        lo += x.shape[0]
    return res


def _pair_rhs(q, in_a):
    zero = jnp.zeros_like(q)
    return jnp.concatenate([jnp.where(in_a, q, zero), jnp.where(in_a, zero, q)], axis=0).astype(BF16)


def _pairmm(p, q, in_a):
    return jnp.dot(p.astype(BF16), _pair_rhs(q, in_a), preferred_element_type=F32)


def _each(fn, *cols):
    return [fn(*xs) for xs in zip(*cols)]


def _time_mix_inputs(h, prev, mu_ref, wrkv_ref, w0_ref, w1_ref, w2_ref, a0_ref, a1_ref, a2_ref, g1_ref, g2_ref,
                     kk_ref, ka_ref, ones2):
    d = h.shape[1]
    xx = prev - h
    mix = lambda i: h + xx * mu_ref[i:i + 1, :]
    r = _dot(mix(0), wrkv_ref[0])
    k = _dot(mix(2), wrkv_ref[1])
    v = _dot(mix(3), wrkv_ref[2])
    wl = w0_ref[...] + _dot(jnp.tanh(_dot(mix(1), w1_ref[...])), w2_ref[...])
    sp = jnp.maximum(-wl, 0.0) + jnp.log(1.0 + jnp.exp(-jnp.abs(wl)))
    logdecay = -jnp.exp(-sp - 0.5)
    a = _sigmoid(a0_ref[...] + _dot(_dot(mix(4), a1_ref[...]), a2_ref[...]))
    g = _dot(_sigmoid(_dot(mix(5), g1_ref[...])), g2_ref[...])

    kk = k * kk_ref[...]
    groups = [kk[:, p * LANES:(p + 1) * LANES] for p in range(d // LANES)]
    ss = _head_sums([x * x for x in groups], ones2)
    kkn = jnp.concatenate([x * lax.rsqrt(jnp.maximum(s, 1e-24)) for x, s in zip(groups, ss)], axis=1)
    return r, logdecay, k * (1.0 + (a - 1.0) * ka_ref[...]), v, -kkn, kkn * a, g


def _wkv_chunks(blk, items, st, rk_ref, lw_ref, lb_ref, ones2):
    shape = (CHUNK, LANES)
    row = lax.broadcasted_iota(jnp.int32, shape, 0)
    lane = lax.broadcasted_iota(jnp.int32, shape, 1)
    in_a = lane < HEAD
    li = lane % HEAD
    strict = li < row
    incl = li <= row
    eye2 = jnp.where(li == row, 1.0, 0.0).astype(F32)
    tri2 = jnp.where(incl, 1.0, 0.0).astype(BF16)
    mm = lambda p_, q_: _pairmm(p_, q_, in_a)

    r, w, k, v, a, b = ([blk(name, it) for it in items] for name in "rwkvab")

    g = _each(lambda x: jnp.dot(tri2, jnp.concatenate(list(_split2(x)), axis=0), preferred_element_type=F32), w)
    e = _each(jnp.exp, g)
    ei = _each(lambda x: jnp.exp(-x), g)
    at = _each(lambda a_, e_, w_: a_ * e_ * jnp.exp(-w_), a, e, w)
    rt = _each(lambda r_, e_: r_ * e_, r, e)
    bt = _each(lambda b_, x: b_ * x, b, ei)
    kt = _each(lambda k_, x: k_ * x, k, ei)
    e_last = _each(lambda e_: e_[CHUNK - 1:CHUNK, :], e)
    bdec = _each(lambda x, el: x * el, bt, e_last)
    kdec = _each(lambda x, el: x * el, kt, e_last)

    def pair_products(at_, rt_, bt_, kt_):
        zero = jnp.zeros_like(bt_)
        rhs4 = jnp.concatenate([jnp.where(in_a, bt_, zero), jnp.where(in_a, zero, bt_),
                                jnp.where(in_a, kt_, zero), jnp.where(in_a, zero, kt_)], axis=0)
        return _dot_nt(jnp.concatenate([at_, rt_], axis=0), rhs4)

    aa = _each(pair_products, at, rt, bt, kt)
    a_ab = _each(lambda x: jnp.where(strict, x[0:CHUNK, 0:LANES], 0.0), aa)
    a_ak = _each(lambda x: jnp.where(strict, x[0:CHUNK, LANES:2 * LANES], 0.0), aa)
    a_rb = _each(lambda x: jnp.where(incl, x[CHUNK:2 * CHUNK, 0:LANES], 0.0), aa)
    a_rk = _each(lambda x: jnp.where(incl, x[CHUNK:2 * CHUNK, LANES:2 * LANES], 0.0), aa)

    def square_and_extend(xp, ti):
        res = jnp.dot(jnp.concatenate([xp, ti], axis=0).astype(BF16), _pair_rhs(xp, in_a),
                      preferred_element_type=F32)
        return res[0:CHUNK], ti + res[CHUNK:2 * CHUNK]

    xpow = _each(mm, a_ab, a_ab)
    tinv = _each(lambda x: eye2 + x, a_ab)
    for _ in range(4):
        both = _each(square_and_extend, xpow, tinv)
        xpow = [x for x, _ in both]
        tinv = [t_ for _, t_ in both]
    tinv = _each(lambda t_, x: t_ + mm(t_, x), tinv, xpow)

    at2 = _each(mm, tinv, at)
    wv = _each(mm, tinv, _each(mm, a_ak, v))
    pn = _each(lambda bd, kd, at2_, wv_, v_: _dot_tn(
        jnp.concatenate([bd, kd], axis=0),
        jnp.concatenate([jnp.concatenate([at2_, wv_], axis=1),
                         jnp.concatenate([jnp.zeros_like(v_), v_], axis=1)], axis=0)), bdec, kdec, at2, wv, v)
    pick = lambda x: jnp.where(in_a, x[0:CHUNK], x[CHUNK:2 * CHUNK])
    p_mat = _each(lambda x: pick(x[:, 0:LANES]), pn)
    n_mat = _each(lambda x: pick(x[:, LANES:2 * LANES]), pn)
    q_mat = _each(lambda rt_, arb, at2_: rt_ + mm(arb, at2_), rt, a_rb, at2)
    y0 = _each(lambda arb, ark, wv_, v_: jnp.dot(
        jnp.concatenate([arb, ark], axis=1).astype(BF16),
        jnp.concatenate([_pair_rhs(wv_, in_a), _pair_rhs(v_, in_a)], axis=0), preferred_element_type=F32),
        a_rb, a_rk, wv, v)
    lanes_of = lambda it: slice(it[1] * LANES, (it[1] + 1) * LANES)
    rk = [rk_ref[:, lanes_of(it)] for it in items]
    sums = _head_sums(_each(lambda el: eye2 * el, e_last) + _each(lambda r_, k_, rk_: r_ * k_ * rk_, r, k, rk),
                      ones2)
    decay, bonus = sums[:len(items)], sums[len(items):]

    y = []
    for i, it in enumerate(items):
        key = it[:2]
        res = jnp.dot(jnp.concatenate([q_mat[i], p_mat[i]], axis=0).astype(BF16), _pair_rhs(st[key], in_a),
                      preferred_element_type=F32)
        y.append(res[0:CHUNK] + y0[i])
        st[key] = decay[i] * st[key] + res[CHUNK:2 * CHUNK] + n_mat[i]

    yc = _each(lambda y_, m_: y_ - m_ * (1.0 / HEAD), y, _head_sums(y, ones2))
    var = _each(lambda x: x * (1.0 / HEAD), _head_sums(_each(lambda x: x * x, yc), ones2))
    return [yc_ * lax.rsqrt(var_ + LNX_EPS) * lw_ref[:, lanes_of(it)] + lb_ref[:, lanes_of(it)] + bonus_ * v_
            for yc_, var_, bonus_, v_, it in zip(yc, var, bonus, v, items)]


def _rwkv_kernel(x_ref, sh0_ref, s0_ref, gain_ref, mu_ref, wrkv_ref, w0_ref, w1_ref, w2_ref, a0_ref, a1_ref, a2_ref,
                 g1_ref, g2_ref, kk_ref, ka_ref, rk_ref, lw_ref, lb_ref,
                 z_ref, g_out, sh_out, s_out, carry_ref, st_ref):
    t = pl.program_id(1)
    n_seq, rows, d = x_ref.shape
    n_sub = rows // CHUNK
    n_pp = d // LANES

    @pl.when(t == 0)
    def _():
        carry_ref[...] = sh0_ref[...]
        st_ref[...] = s0_ref[...]

    x = x_ref[...].reshape(n_seq * rows, d)
    h = _rms(x) * gain_ref[...]
    rolled = pltpu.roll(h, 1, 0)
    first = lax.broadcasted_iota(jnp.int32, (SUBLANES, 1), 0) == 0
    pieces = []
    for q in range(n_seq):
        base = q * rows
        pieces += [jnp.where(first, carry_ref[q], rolled[base:base + SUBLANES]),
                   rolled[base + SUBLANES:base + rows]]
        last = h[base + rows - 1:base + rows, :]
        carry_ref[q] = last
        sh_out[q] = last
    prev = jnp.concatenate(pieces, axis=0)

    ones2 = _head_ones2()
    r, w, k, v, a, b, g = _time_mix_inputs(h, prev, mu_ref, wrkv_ref, w0_ref, w1_ref, w2_ref, a0_ref, a1_ref,
                                           a2_ref, g1_ref, g2_ref, kk_ref, ka_ref, ones2)
    g_out[...] = g.reshape(n_seq, rows, d)

    vals = dict(r=r, w=w, k=k, v=v, a=a, b=b)
    items = [(q, p, s) for s in range(n_sub) for q in range(n_seq) for p in range(n_pp)]

    def blk(name, it):
        q, p, s = it
        lo = q * rows + s * CHUNK
        return vals[name][lo:lo + CHUNK, p * LANES:(p + 1) * LANES]

    st = {(q, p): st_ref[q, :, p * LANES:(p + 1) * LANES] for q in range(n_seq) for p in range(n_pp)}
    z = _wkv_chunks(blk, items, st, rk_ref, lw_ref, lb_ref, ones2)
    for z_, (q, p, s) in zip(z, items):
        z_ref[q, s * CHUNK:(s + 1) * CHUNK, p * LANES:(p + 1) * LANES] = z_
    for (q, p), val in st.items():
        st_ref[q, :, p * LANES:(p + 1) * LANES] = val

    @pl.when(t == pl.num_programs(1) - 1)
    def _():
        for (q, p), val in st.items():
            s_out[q, :, p * LANES:(p + 1) * LANES] = val


def _rwkv_layer(x, shift0, st0, gain, mu, wrkv, w0, w1, w2, a0, a1, a2, g1, g2, k_k, k_a, r_k, lnx_w, lnx_b,
                n_seq, n_sub):
    bsz, t_len, d = x.shape
    rows = n_sub * CHUNK
    row = lambda arr: arr.reshape(1, d)
    const = lambda arr: _resident(arr.shape, lambda b, t: (0,) * arr.ndim)
    seq = pl.BlockSpec((n_seq, rows, d), lambda b, t: (b, t, 0))
    per_seq = lambda n: pl.BlockSpec((n_seq, n, d), lambda b, t: (b, 0, 0))
    args = (x, shift0.reshape(bsz, 1, d), st0, row(gain), mu, wrkv, row(w0), w1, w2, row(a0), a1, a2, g1, g2,
            row(k_k), row(k_a), row(r_k), row(lnx_w), row(lnx_b))
    act = jax.ShapeDtypeStruct((bsz, t_len, d), F32)
    return pl.pallas_call(
        _rwkv_kernel,
        grid=(bsz // n_seq, t_len // rows),
        in_specs=[seq, per_seq(1), per_seq(HEAD)] + [const(a) for a in args[3:]],
        out_specs=[seq, seq, per_seq(1), per_seq(HEAD)],
        out_shape=[act, act, jax.ShapeDtypeStruct((bsz, 1, d), F32), jax.ShapeDtypeStruct((bsz, HEAD, d), F32)],
        scratch_shapes=[pltpu.VMEM((n_seq, 1, d), F32), pltpu.VMEM((n_seq, HEAD, d), F32)],
        compiler_params=pltpu.CompilerParams(dimension_semantics=("arbitrary", "arbitrary"),
                                             vmem_limit_bytes=VMEM_LIMIT_BYTES),
    )(*args)


def _swiglu_residual(x, gain, win_ref, wout_ref, ff_chunk):
    n_ff = wout_ref.shape[0]
    hn = (_rms(x) * gain).astype(BF16)
    acc = x
    for j in range(n_ff // ff_chunk):
        lo = j * ff_chunk
        gate = jnp.dot(hn, win_ref[:, lo:lo + ff_chunk], preferred_element_type=F32)
        up = jnp.dot(hn, win_ref[:, n_ff + lo:n_ff + lo + ff_chunk], preferred_element_type=F32)
        hid = gate * _sigmoid(gate) * up
        acc = acc + jnp.dot(hid.astype(BF16), wout_ref[lo:lo + ff_chunk, :], preferred_element_type=F32)
    return acc


def _rope(x, cs, sn, first):
    out = []
    for p in range(x.shape[1] // LANES):
        xp = x[:, p * LANES:(p + 1) * LANES]
        partner = jnp.where(first, pltpu.roll(xp, LANES - ROT_DIM // 2, 1), pltpu.roll(xp, ROT_DIM // 2, 1))
        out.append(xp * cs + partner * sn)
    return jnp.concatenate(out, axis=1)


def _ffn_qkv_kernel(z_ref, g_ref, x_ref, wo_ref, gain_ref, win_ref, wout_ref, gq_ref, gkv_ref, wq_ref, bq_ref,
                    wkv_ref, bkv_ref, inv_ref, x_out, q_out, k_out, v_out, *, ff_chunk, seq_len, pos_base):
    x1 = x_ref[...] + _dot(z_ref[...] * g_ref[...], wo_ref[...])
    x2 = _swiglu_residual(x1, gain_ref[...], win_ref, wout_ref, ff_chunk)
    x_out[...] = x2

    tm = x2.shape[0]
    kvw = k_out.shape[1]
    y = _rms(x2)
    q = _dot(y * gq_ref[...], wq_ref[...]) + bq_ref[...]
    kv = _dot(y * gkv_ref[...], wkv_ref[...]) + bkv_ref[...]
    rows = lax.broadcasted_iota(jnp.int32, (tm, 1), 0) + pl.program_id(0) * tm
    pos = (lax.rem(rows, seq_len) + pos_base).astype(F32)
    ang = pos * inv_ref[...]
    lane = lax.broadcasted_iota(jnp.int32, (1, LANES), 1) % HEAD
    first = lane < ROT_DIM // 2
    cs = jnp.cos(ang)
    sn = jnp.where(first, -1.0, 1.0) * jnp.sin(ang)
    q_out[...] = _rope(q, cs, sn, first)
    k_out[...] = _rope(kv[:, :kvw], cs, sn, first)
    v_out[...] = kv[:, kvw:]


def _ffn_qkv(z, g, x, w_o, gain, w_in, w_out, gain_q, gain_kv, w_q, b_q, w_kv, b_kv, seq_len, pos_base, tm,
             ff_chunk):
    n, d = x.shape
    kvw = w_kv.shape[1] // 2
    half = ROT_DIM // 2
    lane = jnp.arange(LANES) % HEAD
    inv = jnp.power(jnp.float32(ROPE_THETA), -(lane % half).astype(F32) * (2.0 / ROT_DIM))
    inv = jnp.where(lane < ROT_DIM, inv, 0.0).reshape(1, LANES)
    tile = lambda w: pl.BlockSpec((tm, w), lambda i: (i, 0))
    const = lambda arr: _resident(arr.shape, lambda i: (0, 0))
    consts = (w_o, gain.reshape(1, d), w_in, w_out, gain_q.reshape(1, d), gain_kv.reshape(1, d), w_q,
              b_q.reshape(1, -1), w_kv, b_kv.reshape(1, -1), inv)
    qw = w_q.shape[1]
    return pl.pallas_call(
        functools.partial(_ffn_qkv_kernel, ff_chunk=ff_chunk, seq_len=seq_len, pos_base=pos_base),
        grid=(n // tm,),
        in_specs=[tile(d)] * 3 + [const(a) for a in consts],
        out_specs=[tile(d), tile(qw), tile(kvw), tile(kvw)],
        out_shape=[jax.ShapeDtypeStruct((n, d), F32), jax.ShapeDtypeStruct((n, qw), F32),
                   jax.ShapeDtypeStruct((n, kvw), F32), jax.ShapeDtypeStruct((n, kvw), F32)],
        compiler_params=pltpu.CompilerParams(dimension_semantics=("arbitrary",),
                                             vmem_limit_bytes=VMEM_LIMIT_BYTES),
    )(z, g, x, *consts)


def _dup_heads(kp, in_a):
    rolled = pltpu.roll(kp, HEAD, 1)
    return jnp.where(in_a, kp, rolled), jnp.where(in_a, rolled, kp)


def _attend(q_chunks, keys, vals, valids, sink_ref):
    n_kv = len(keys[0])
    groups_per_kv = q_chunks[0].shape[1] // LANES // n_kv
    in_a = lax.broadcasted_iota(jnp.int32, (CHUNK, LANES), 1) < HEAD
    ones = jnp.ones((keys[0][0].shape[0], LANES), BF16)
    inst = [(c, hk) for c in range(len(q_chunks)) for hk in range(n_kv)]

    sink_cols = []
    for hk in range(n_kv):
        cols = [jnp.full((CHUNK, 1), sink_ref[hk * 2 * groups_per_kv + j], F32) for j in range(2 * groups_per_kv)]
        sink_cols.append(jnp.concatenate(cols, axis=0))

    def scores(c, hk):
        blocks = []
        for m in range(groups_per_kv):
            lo = (hk * groups_per_kv + m) * LANES
            qg = q_chunks[c][:, lo:lo + LANES] * ATTN_SCALE
            zero = jnp.zeros_like(qg)
            blocks += [jnp.where(in_a, qg, zero), jnp.where(in_a, zero, qg)]
        s = _dot_nt(jnp.concatenate(blocks, axis=0), keys[c][hk])
        return s if valids[c] is None else jnp.where(valids[c], s, NEG_INF)

    s = [scores(c, hk) for c, hk in inst]
    m_row = [jnp.maximum(jnp.max(x, axis=-1, keepdims=True), sink_cols[hk]) for x, (c, hk) in zip(s, inst)]
    p = [jnp.exp(x - m).astype(BF16) for x, m in zip(s, m_row)]
    pv = [jnp.dot(x, jnp.concatenate([vals[c][hk].astype(BF16), ones], axis=1), preferred_element_type=F32)
          for x, (c, hk) in zip(p, inst)]
    o = [x[:, :LANES] / (x[:, LANES:] + jnp.exp(sink_cols[hk] - m)) for x, m, (c, hk) in zip(pv, m_row, inst)]
    rows = []
    for c in range(len(q_chunks)):
        groups = []
        for hk in range(n_kv):
            oc = o[c * n_kv + hk]
            for m in range(groups_per_kv):
                groups.append(jnp.where(in_a, oc[2 * m * CHUNK:(2 * m + 1) * CHUNK],
                                        oc[(2 * m + 1) * CHUNK:(2 * m + 2) * CHUNK]))
        rows.append(jnp.concatenate(groups, axis=1))
    return jnp.concatenate(rows, axis=0)


def _attn_ffn_kernel(sink_ref, q_ref, kc_ref, vc_ref, kp_ref, vp_ref, x_ref, wo_ref, bo_ref, gain_ref,
                     win_ref, wout_ref, gfin_ref, o_ref, *, streaming, ff_chunk):
    tm = x_ref.shape[-2]
    n_chunks = tm // CHUNK
    kvw = kc_ref.shape[-1]
    n_keys = WINDOW + CHUNK

    def dup(arr):
        in_a = lax.broadcasted_iota(jnp.int32, (arr.shape[0], LANES), 1) < HEAD
        res = []
        for j in range(kvw // LANES):
            res += list(_dup_heads(arr[:, j * LANES:(j + 1) * LANES], in_a))
        return res

    if streaming:
        q = q_ref[0]
        k_all = dup(jnp.concatenate([kp_ref[0], kc_ref[0]], axis=0))
        v_all = dup(jnp.concatenate([vp_ref[0], vc_ref[0]], axis=0))
        first_tile = pl.program_id(1) == 0
        key_idx = lax.broadcasted_iota(jnp.int32, (1, n_keys), 1)
    else:
        q = q_ref[...]

    q_chunks, keys, vals, valids = [], [], [], []
    for c in range(n_chunks):
        q_chunks.append(q[c * CHUNK:(c + 1) * CHUNK])
        valid = None
        if streaming:
            keys.append([ka[c * CHUNK:c * CHUNK + n_keys] for ka in k_all])
            vals.append([va[c * CHUNK:c * CHUNK + n_keys] for va in v_all])
            if c * CHUNK < WINDOW:
                valid = jnp.logical_or(key_idx + c * CHUNK >= WINDOW, jnp.logical_not(first_tile))
        else:
            keys.append(dup(jnp.concatenate([kp_ref[c], kc_ref[c * CHUNK:(c + 1) * CHUNK]], axis=0)))
            vals.append(dup(jnp.concatenate([vp_ref[c], vc_ref[c * CHUNK:(c + 1) * CHUNK]], axis=0)))
        valids.append(valid)
    o = _attend(q_chunks, keys, vals, valids, sink_ref)

    x = x_ref[0] if streaming else x_ref[...]
    x1 = x + _dot(o, wo_ref[...]) + bo_ref[...]
    x2 = _swiglu_residual(x1, gain_ref[...], win_ref, wout_ref, ff_chunk)
    y = _rms(x2) * gfin_ref[...]
    if streaming:
        o_ref[0] = y
    else:
        o_ref[...] = y


def _attn_ffn(q, k_new, v_new, k_prev, v_prev, x, sinks, w_o, b_o, gain, w_in, w_out, gain_fin, tm, ff_chunk):
    d = x.shape[-1]
    kvw = k_new.shape[-1]
    streaming = k_prev is None
    smem = pl.BlockSpec(memory_space=pltpu.SMEM)
    weights = (w_o, b_o.reshape(1, d), gain.reshape(1, d), w_in, w_out, gain_fin.reshape(1, d))
    if streaming:
        bsz, t_len, _ = x.shape
        assert tm % WINDOW == 0
        per_win = tm // WINDOW
        grid = (bsz, t_len // tm)
        tile = lambda w: pl.BlockSpec((1, tm, w), lambda b, t: (b, t, 0))
        prev = pl.BlockSpec((1, WINDOW, kvw), lambda b, t: (b, jnp.maximum(t * per_win - 1, 0), 0))
        const = lambda arr: _resident(arr.shape, lambda b, t: (0, 0))
        in_specs = [smem, tile(d), tile(kvw), tile(kvw), prev, prev, tile(d)] + [const(a) for a in weights]
        args = (sinks, q, k_new, v_new, k_new, v_new, x) + weights
        out_spec = tile(d)
        sem = ("arbitrary", "arbitrary")
    else:
        n = x.shape[0]
        grid = (1,)
        full = lambda arr: pl.BlockSpec(arr.shape, lambda i: (0,) * arr.ndim)
        args = (sinks, q, k_new, v_new, k_prev, v_prev, x) + weights
        in_specs = [smem] + [full(a) for a in args[1:]]
        out_spec = pl.BlockSpec((n, d), lambda i: (0, 0))
        sem = ("arbitrary",)
    return pl.pallas_call(
        functools.partial(_attn_ffn_kernel, streaming=streaming, ff_chunk=ff_chunk),
        grid=grid,
        in_specs=in_specs,
        out_specs=out_spec,
        out_shape=jax.ShapeDtypeStruct(x.shape, F32),
        compiler_params=pltpu.CompilerParams(dimension_semantics=sem, vmem_limit_bytes=VMEM_LIMIT_BYTES),
    )(*args)


def _tile(n, cap):
    t = min(n, cap)
    while n % t:
        t -= 1
    return t


def kernel(x_prompt, x_sample, state_wkv, state_shift, cache_k, cache_v, norm_mix, norm_ffn, rw_mu, rw_w_rkv, rw_w0, rw_w1, rw_w2, rw_a0, rw_a1, rw_a2, rw_g1, rw_g2, rw_k_k, rw_k_a, rw_r_k, rw_lnx_w, rw_lnx_b, rw_w_o, kv_norm, w_kv, b_kv, w_q, b_q, attn_sinks, w_o, b_o, ffn_w_in, ffn_w_out, norm_final):
    depth, d = norm_mix.shape
    assert depth == 2 and rw_mu.shape[0] == 1 and w_q.shape[0] == 1, "one RWKV layer followed by one attention layer"
    n_heads = d // HEAD
    n_ff = ffn_w_out.shape[1]
    ff_chunk = 256 if n_ff % 256 == 0 else n_ff
    bf = lambda w: w.astype(BF16)
    wrkv, w1, w2, a1, a2, g1, g2 = map(bf, (rw_w_rkv[0], rw_w1[0], rw_w2[0], rw_a1[0], rw_a2[0], rw_g1[0], rw_g2[0]))
    rw_wo, wq, wkv, wo = bf(rw_w_o[0]), bf(w_q[0]), bf(w_kv), bf(w_o[0])
    win, wout = bf(ffn_w_in), bf(ffn_w_out)
    sinks = attn_sinks[0].reshape(-1)

    def run(x, pos_base, shift0, st0, win_k, win_v):
        bsz, t_len, _ = x.shape
        n_rows = bsz * t_len
        n_sub = _tile(t_len // CHUNK, 4)
        z, g, shift, st = _rwkv_layer(
            x, shift0, st0, norm_mix[0], rw_mu[0], wrkv, rw_w0[0], w1, w2, rw_a0[0], a1, a2, g1, g2,
            rw_k_k[0], rw_k_a[0], rw_r_k[0].reshape(-1), rw_lnx_w[0], rw_lnx_b[0],
            n_seq=_tile(bsz, 4 // n_sub), n_sub=n_sub)
        flat = lambda t: t.reshape(n_rows, t.shape[-1])
        x1, q, k_new, v_new = _ffn_qkv(
            flat(z), flat(g), flat(x), rw_wo, norm_ffn[0], win[0], wout[0], norm_mix[1], kv_norm, wq, b_q[0],
            wkv, b_kv, t_len, pos_base, tm=_tile(n_rows, 512), ff_chunk=ff_chunk)
        kvw = k_new.shape[-1]
        k_new = k_new.reshape(bsz, t_len, kvw)
        v_new = v_new.reshape(bsz, t_len, kvw)
        if win_k is None:
            y = _attn_ffn(q.reshape(bsz, t_len, d), k_new, v_new, None, None, x1.reshape(bsz, t_len, d), sinks,
                          wo, b_o[0], norm_ffn[1], win[1], wout[1], norm_final, tm=_tile(t_len, 512),
                          ff_chunk=ff_chunk)
            k_state, v_state = k_new[:, -WINDOW:], v_new[:, -WINDOW:]
        else:
            assert t_len == CHUNK
            kp = win_k.reshape(bsz, WINDOW, kvw)
            vp = win_v.reshape(bsz, WINDOW, kvw)
            y = _attn_ffn(q, k_new.reshape(n_rows, kvw), v_new.reshape(n_rows, kvw), kp, vp, x1, sinks,
                          wo, b_o[0], norm_ffn[1], win[1], wout[1], norm_final, tm=n_rows, ff_chunk=ff_chunk)
            k_state = jnp.concatenate([kp, k_new], axis=1)[:, -WINDOW:]
            v_state = jnp.concatenate([vp, v_new], axis=1)[:, -WINDOW:]
        n_kv = kvw // HEAD
        st = st.reshape(bsz, HEAD, n_heads, HEAD).transpose(0, 2, 3, 1)[None]
        return (y.reshape(bsz, t_len, d), st, shift.reshape(1, bsz, d),
                k_state.reshape(bsz, WINDOW, n_kv, HEAD), v_state.reshape(bsz, WINDOW, n_kv, HEAD))

    bp = x_prompt.shape[0]
    zeros_state = jnp.zeros((bp, HEAD, d), F32)
    y_p, wkv_p, shift_p, k_p, v_p = run(x_prompt, 0, jnp.zeros((bp, d), F32), zeros_state, None, None)
    st_s = state_wkv[0].transpose(0, 3, 1, 2).reshape(x_sample.shape[0], HEAD, d)
    y_s, wkv_s, shift_s, k_s, v_s = run(x_sample, PAST_LEN, state_shift[0], st_s, cache_k, cache_v)
    return (y_p, y_s, wkv_p, shift_p, k_p, v_p, wkv_s, shift_s, k_s, v_s)
```

```python
import functools

import jax
import jax.numpy as jnp
from jax import lax
from jax.experimental import pallas as pl
from jax.experimental.pallas import tpu as pltpu

F32 = jnp.float32
BF16 = jnp.bfloat16

HEAD = 64
LANES = 128
SUBLANES = 8
CHUNK = 64
WINDOW = 128
ROT_DIM = HEAD // 4
ROPE_THETA = 500000.0
RMS_EPS = 1e-5
LNX_EPS = 64e-5
NEG_INF = -1e30
ATTN_SCALE = 1.0 / (HEAD ** 0.5)
VMEM_LIMIT_BYTES = 56 * 1024 * 1024
PAST_LEN = 1024


def _resident(shape, index_map):
    return pl.BlockSpec(shape, index_map, pipeline_mode=pl.Buffered(1))


def _dot(a, b):
    return jnp.dot(a.astype(BF16), b.astype(BF16), preferred_element_type=F32)


def _dot_nt(a, b):
    return lax.dot_general(a.astype(BF16), b.astype(BF16), (((1,), (1,)), ((), ())),
                           preferred_element_type=F32)


def _dot_tn(a, b):
    return lax.dot_general(a.astype(BF16), b.astype(BF16), (((0,), (0,)), ((), ())),
                           preferred_element_type=F32)


def _split2(x):
    hi = x.astype(BF16)
    return hi, (x - hi.astype(F32)).astype(BF16)


def _rms(x):
    return x * lax.rsqrt(jnp.mean(x * x, axis=-1, keepdims=True) + RMS_EPS)


def _sigmoid(x):
    return 1.0 / (1.0 + jnp.exp(-x))


def _head_ones2():
    r = lax.broadcasted_iota(jnp.int32, (2 * LANES, LANES), 0)
    c = lax.broadcasted_iota(jnp.int32, (2 * LANES, LANES), 1)
    return jnp.where(((r % LANES) // HEAD) == (c // HEAD), 1.0, 0.0).astype(BF16)


def _head_sums(xs, ones2):
    lhs = jnp.concatenate([jnp.concatenate(list(_split2(x)), axis=1) for x in xs], axis=0)
    out = jnp.dot(lhs, ones2, preferred_element_type=F32)
    res, lo = [], 0
    for x in xs:
        res.append(out[lo:lo + x.shape[0]])
        lo += x.shape[0]
    return res


def _pair_rhs(q, in_a):
    zero = jnp.zeros_like(q)
    return jnp.concatenate([jnp.where(in_a, q, zero), jnp.where(in_a, zero, q)], axis=0).astype(BF16)


def _pairmm(p, q, in_a):
    return jnp.dot(p.astype(BF16), _pair_rhs(q, in_a), preferred_element_type=F32)


def _each(fn, *cols):
    return [fn(*xs) for xs in zip(*cols)]


def _decay_and_rate(mix, w0_ref, w1_ref, w2_ref, a0_ref, a1_ref, a2_ref):
    wl = w0_ref[...] + _dot(jnp.tanh(_dot(mix(1), w1_ref[...])), w2_ref[...])
    sp = jnp.maximum(-wl, 0.0) + jnp.log(1.0 + jnp.exp(-jnp.abs(wl)))
    logdecay = -jnp.exp(-sp - 0.5)
    a = _sigmoid(a0_ref[...] + _dot(_dot(mix(4), a1_ref[...]), a2_ref[...]))
    return logdecay, a


def _decay_factors(w_blocks):
    shape = (CHUNK, LANES)
    incl = lax.broadcasted_iota(jnp.int32, shape, 1) % HEAD <= lax.broadcasted_iota(jnp.int32, shape, 0)
    tri2 = jnp.where(incl, 1.0, 0.0).astype(BF16)
    g = _each(lambda x: jnp.dot(tri2, jnp.concatenate(list(_split2(x)), axis=0), preferred_element_type=F32),
              w_blocks)
    return _each(jnp.exp, g), _each(lambda x: jnp.exp(-x), g), _each(lambda x: jnp.exp(-x), w_blocks)


def _receptance_key_value_gate(mix, a, wrkv_ref, g1_ref, g2_ref, kk_ref, ka_ref, ones2):
    k = _dot(mix(2), wrkv_ref[1])
    r = _dot(mix(0), wrkv_ref[0])
    v = _dot(mix(3), wrkv_ref[2])
    g = _dot(_sigmoid(_dot(mix(5), g1_ref[...])), g2_ref[...])
    kk = k * kk_ref[...]
    groups = [kk[:, p * LANES:(p + 1) * LANES] for p in range(k.shape[1] // LANES)]
    ss = _head_sums([x * x for x in groups], ones2)
    kkn = jnp.concatenate([x * lax.rsqrt(jnp.maximum(s, 1e-24)) for x, s in zip(groups, ss)], axis=1)
    return r, k * (1.0 + (a - 1.0) * ka_ref[...]), v, -kkn, kkn * a, g


def _wkv_chunks(blk, items, decay_factors, st, rk_ref, lw_ref, lb_ref, ones2):
    shape = (CHUNK, LANES)
    row = lax.broadcasted_iota(jnp.int32, shape, 0)
    lane = lax.broadcasted_iota(jnp.int32, shape, 1)
    in_a = lane < HEAD
    li = lane % HEAD
    strict = li < row
    incl = li <= row
    eye2 = jnp.where(li == row, 1.0, 0.0).astype(F32)
    mm = lambda p_, q_: _pairmm(p_, q_, in_a)

    r, k, v, a, b = ([blk(name, it) for it in items] for name in "rkvab")
    e, ei, ew = decay_factors
    at = _each(lambda a_, e_, ew_: a_ * e_ * ew_, a, e, ew)
    rt = _each(lambda r_, e_: r_ * e_, r, e)
    bt = _each(lambda b_, x: b_ * x, b, ei)
    kt = _each(lambda k_, x: k_ * x, k, ei)
    e_last = _each(lambda e_: e_[CHUNK - 1:CHUNK, :], e)
    bdec = _each(lambda x, el: x * el, bt, e_last)
    kdec = _each(lambda x, el: x * el, kt, e_last)

    def pair_products(at_, rt_, bt_, kt_):
        zero = jnp.zeros_like(bt_)
        rhs4 = jnp.concatenate([jnp.where(in_a, bt_, zero), jnp.where(in_a, zero, bt_),
                                jnp.where(in_a, kt_, zero), jnp.where(in_a, zero, kt_)], axis=0)
        return _dot_nt(jnp.concatenate([at_, rt_], axis=0), rhs4)

    aa = _each(pair_products, at, rt, bt, kt)
    a_ab = _each(lambda x: jnp.where(strict, x[0:CHUNK, 0:LANES], 0.0), aa)
    a_ak = _each(lambda x: jnp.where(strict, x[0:CHUNK, LANES:2 * LANES], 0.0), aa)
    a_rb = _each(lambda x: jnp.where(incl, x[CHUNK:2 * CHUNK, 0:LANES], 0.0), aa)
    a_rk = _each(lambda x: jnp.where(incl, x[CHUNK:2 * CHUNK, LANES:2 * LANES], 0.0), aa)

    def square_and_extend(xp, ti):
        res = jnp.dot(jnp.concatenate([xp, ti], axis=0).astype(BF16), _pair_rhs(xp, in_a),
                      preferred_element_type=F32)
        return res[0:CHUNK], ti + res[CHUNK:2 * CHUNK]

    xpow = _each(mm, a_ab, a_ab)
    tinv = _each(lambda x: eye2 + x, a_ab)
    for _ in range(4):
        both = _each(square_and_extend, xpow, tinv)
        xpow = [x for x, _ in both]
        tinv = [t_ for _, t_ in both]
    tinv = _each(lambda t_, x: t_ + mm(t_, x), tinv, xpow)

    at2 = _each(mm, tinv, at)
    wv = _each(mm, tinv, _each(mm, a_ak, v))
    pn = _each(lambda bd, kd, at2_, wv_, v_: _dot_tn(
        jnp.concatenate([bd, kd], axis=0),
        jnp.concatenate([jnp.concatenate([at2_, wv_], axis=1),
                         jnp.concatenate([jnp.zeros_like(v_), v_], axis=1)], axis=0)), bdec, kdec, at2, wv, v)
    pick = lambda x: jnp.where(in_a, x[0:CHUNK], x[CHUNK:2 * CHUNK])
    p_mat = _each(lambda x: pick(x[:, 0:LANES]), pn)
    n_mat = _each(lambda x: pick(x[:, LANES:2 * LANES]), pn)
    q_mat = _each(lambda rt_, arb, at2_: rt_ + mm(arb, at2_), rt, a_rb, at2)
    y0 = _each(lambda arb, ark, wv_, v_: jnp.dot(
        jnp.concatenate([arb, ark], axis=1).astype(BF16),
        jnp.concatenate([_pair_rhs(wv_, in_a), _pair_rhs(v_, in_a)], axis=0), preferred_element_type=F32),
        a_rb, a_rk, wv, v)
    lanes_of = lambda it: slice(it[1] * LANES, (it[1] + 1) * LANES)
    rk = [rk_ref[:, lanes_of(it)] for it in items]
    sums = _head_sums(_each(lambda el: eye2 * el, e_last) + _each(lambda r_, k_, rk_: r_ * k_ * rk_, r, k, rk),
                      ones2)
    decay, bonus = sums[:len(items)], sums[len(items):]

    y = []
    for i, it in enumerate(items):
        key = it[:2]
        res = jnp.dot(jnp.concatenate([q_mat[i], p_mat[i]], axis=0).astype(BF16), _pair_rhs(st[key], in_a),
                      preferred_element_type=F32)
        y.append(res[0:CHUNK] + y0[i])
        st[key] = decay[i] * st[key] + res[CHUNK:2 * CHUNK] + n_mat[i]

    yc = _each(lambda y_, m_: y_ - m_ * (1.0 / HEAD), y, _head_sums(y, ones2))
    var = _each(lambda x: x * (1.0 / HEAD), _head_sums(_each(lambda x: x * x, yc), ones2))
    return [yc_ * lax.rsqrt(var_ + LNX_EPS) * lw_ref[:, lanes_of(it)] + lb_ref[:, lanes_of(it)] + bonus_ * v_
            for yc_, var_, bonus_, v_, it in zip(yc, var, bonus, v, items)]


def _rwkv_kernel(x_ref, sh0_ref, s0_ref, gain_ref, mu_ref, wrkv_ref, w0_ref, w1_ref, w2_ref, a0_ref, a1_ref, a2_ref,
                 g1_ref, g2_ref, kk_ref, ka_ref, rk_ref, lw_ref, lb_ref,
                 z_ref, g_out, sh_out, s_out, carry_ref, st_ref):
    t = pl.program_id(1)
    n_seq, rows, d = x_ref.shape
    n_sub = rows // CHUNK
    n_pp = d // LANES

    @pl.when(t == 0)
    def _():
        carry_ref[...] = sh0_ref[...]
        st_ref[...] = s0_ref[...]

    x = x_ref[...].reshape(n_seq * rows, d)
    h = _rms(x) * gain_ref[...]
    rolled = pltpu.roll(h, 1, 0)
    first = lax.broadcasted_iota(jnp.int32, (SUBLANES, 1), 0) == 0
    pieces = []
    for q in range(n_seq):
        base = q * rows
        pieces += [jnp.where(first, carry_ref[q], rolled[base:base + SUBLANES]),
                   rolled[base + SUBLANES:base + rows]]
        last = h[base + rows - 1:base + rows, :]
        carry_ref[q] = last
        sh_out[q] = last
    prev = jnp.concatenate(pieces, axis=0)

    ones2 = _head_ones2()
    items = [(q, p, s) for s in range(n_sub) for q in range(n_seq) for p in range(n_pp)]
    vals = {}

    def blk(name, it):
        q, p, s = it
        lo = q * rows + s * CHUNK
        return vals[name][lo:lo + CHUNK, p * LANES:(p + 1) * LANES]

    xx = prev - h
    mix = lambda i: h + xx * mu_ref[i:i + 1, :]
    vals["w"], a_rate = _decay_and_rate(mix, w0_ref, w1_ref, w2_ref, a0_ref, a1_ref, a2_ref)
    vals["r"], vals["k"], vals["v"], vals["a"], vals["b"], g = _receptance_key_value_gate(
        mix, a_rate, wrkv_ref, g1_ref, g2_ref, kk_ref, ka_ref, ones2)
    g_out[...] = g.reshape(n_seq, rows, d)
    decay_factors = _decay_factors([blk("w", it) for it in items])

    st = {(q, p): st_ref[q, :, p * LANES:(p + 1) * LANES] for q in range(n_seq) for p in range(n_pp)}
    z = _wkv_chunks(blk, items, decay_factors, st, rk_ref, lw_ref, lb_ref, ones2)
    for z_, (q, p, s) in zip(z, items):
        z_ref[q, s * CHUNK:(s + 1) * CHUNK, p * LANES:(p + 1) * LANES] = z_
    for (q, p), val in st.items():
        st_ref[q, :, p * LANES:(p + 1) * LANES] = val

    @pl.when(t == pl.num_programs(1) - 1)
    def _():
        for (q, p), val in st.items():
            s_out[q, :, p * LANES:(p + 1) * LANES] = val


def _rwkv_layer(x, shift0, st0, gain, mu, wrkv, w0, w1, w2, a0, a1, a2, g1, g2, k_k, k_a, r_k, lnx_w, lnx_b,
                n_seq, n_sub):
    bsz, t_len, d = x.shape
    rows = n_sub * CHUNK
    row = lambda arr: arr.reshape(1, d)
    const = lambda arr: _resident(arr.shape, lambda b, t: (0,) * arr.ndim)
    seq = pl.BlockSpec((n_seq, rows, d), lambda b, t: (b, t, 0))
    per_seq = lambda n: pl.BlockSpec((n_seq, n, d), lambda b, t: (b, 0, 0))
    args = (x, shift0.reshape(bsz, 1, d), st0, row(gain), mu, wrkv, row(w0), w1, w2, row(a0), a1, a2, g1, g2,
            row(k_k), row(k_a), row(r_k), row(lnx_w), row(lnx_b))
    act = jax.ShapeDtypeStruct((bsz, t_len, d), F32)
    return pl.pallas_call(
        _rwkv_kernel,
        grid=(bsz // n_seq, t_len // rows),
        in_specs=[seq, per_seq(1), per_seq(HEAD)] + [const(a) for a in args[3:]],
        out_specs=[seq, seq, per_seq(1), per_seq(HEAD)],
        out_shape=[act, act, jax.ShapeDtypeStruct((bsz, 1, d), F32), jax.ShapeDtypeStruct((bsz, HEAD, d), F32)],
        scratch_shapes=[pltpu.VMEM((n_seq, 1, d), F32), pltpu.VMEM((n_seq, HEAD, d), F32)],
        compiler_params=pltpu.CompilerParams(dimension_semantics=("arbitrary", "arbitrary"),
                                             vmem_limit_bytes=VMEM_LIMIT_BYTES),
    )(*args)


def _swiglu_residual(x, gain, win_ref, wout_ref, ff_chunk):
    n_ff = wout_ref.shape[0]
    hn = (_rms(x) * gain).astype(BF16)
    acc = x
    for j in range(n_ff // ff_chunk):
        lo = j * ff_chunk
        gate = jnp.dot(hn, win_ref[:, lo:lo + ff_chunk], preferred_element_type=F32)
        up = jnp.dot(hn, win_ref[:, n_ff + lo:n_ff + lo + ff_chunk], preferred_element_type=F32)
        hid = gate * _sigmoid(gate) * up
        acc = acc + jnp.dot(hid.astype(BF16), wout_ref[lo:lo + ff_chunk, :], preferred_element_type=F32)
    return acc


def _rope(x, cs, sn, first):
    out = []
    for p in range(x.shape[1] // LANES):
        xp = x[:, p * LANES:(p + 1) * LANES]
        partner = jnp.where(first, pltpu.roll(xp, LANES - ROT_DIM // 2, 1), pltpu.roll(xp, ROT_DIM // 2, 1))
        out.append(xp * cs + partner * sn)
    return jnp.concatenate(out, axis=1)


def _ffn_qkv_kernel(z_ref, g_ref, x_ref, wo_ref, gain_ref, win_ref, wout_ref, gq_ref, gkv_ref, wq_ref, bq_ref,
                    wkv_ref, bkv_ref, inv_ref, x_out, q_out, k_out, v_out, *, ff_chunk, seq_len, pos_base):
    x1 = x_ref[...] + _dot(z_ref[...] * g_ref[...], wo_ref[...])
    x2 = _swiglu_residual(x1, gain_ref[...], win_ref, wout_ref, ff_chunk)
    x_out[...] = x2

    tm = x2.shape[0]
    kvw = k_out.shape[1]
    y = _rms(x2)
    q = _dot(y * gq_ref[...], wq_ref[...]) + bq_ref[...]
    kv = _dot(y * gkv_ref[...], wkv_ref[...]) + bkv_ref[...]
    rows = lax.broadcasted_iota(jnp.int32, (tm, 1), 0) + pl.program_id(0) * tm
    pos = (lax.rem(rows, seq_len) + pos_base).astype(F32)
    ang = pos * inv_ref[...]
    lane = lax.broadcasted_iota(jnp.int32, (1, LANES), 1) % HEAD
    first = lane < ROT_DIM // 2
    cs = jnp.cos(ang)
    sn = jnp.where(first, -1.0, 1.0) * jnp.sin(ang)
    q_out[...] = _rope(q, cs, sn, first)
    k_out[...] = _rope(kv[:, :kvw], cs, sn, first)
    v_out[...] = kv[:, kvw:]


def _ffn_qkv(z, g, x, w_o, gain, w_in, w_out, gain_q, gain_kv, w_q, b_q, w_kv, b_kv, seq_len, pos_base, tm,
             ff_chunk):
    n, d = x.shape
    kvw = w_kv.shape[1] // 2
    half = ROT_DIM // 2
    lane = jnp.arange(LANES) % HEAD
    inv = jnp.power(jnp.float32(ROPE_THETA), -(lane % half).astype(F32) * (2.0 / ROT_DIM))
    inv = jnp.where(lane < ROT_DIM, inv, 0.0).reshape(1, LANES)
    tile = lambda w: pl.BlockSpec((tm, w), lambda i: (i, 0))
    const = lambda arr: _resident(arr.shape, lambda i: (0, 0))
    consts = (w_o, gain.reshape(1, d), w_in, w_out, gain_q.reshape(1, d), gain_kv.reshape(1, d), w_q,
              b_q.reshape(1, -1), w_kv, b_kv.reshape(1, -1), inv)
    qw = w_q.shape[1]
    return pl.pallas_call(
        functools.partial(_ffn_qkv_kernel, ff_chunk=ff_chunk, seq_len=seq_len, pos_base=pos_base),
        grid=(n // tm,),
        in_specs=[tile(d)] * 3 + [const(a) for a in consts],
        out_specs=[tile(d), tile(qw), tile(kvw), tile(kvw)],
        out_shape=[jax.ShapeDtypeStruct((n, d), F32), jax.ShapeDtypeStruct((n, qw), F32),
                   jax.ShapeDtypeStruct((n, kvw), F32), jax.ShapeDtypeStruct((n, kvw), F32)],
        compiler_params=pltpu.CompilerParams(dimension_semantics=("arbitrary",),
                                             vmem_limit_bytes=VMEM_LIMIT_BYTES),
    )(z, g, x, *consts)


def _dup_heads(kp, in_a):
    rolled = pltpu.roll(kp, HEAD, 1)
    return jnp.where(in_a, kp, rolled), jnp.where(in_a, rolled, kp)


def _attend(q_chunks, keys, vals, valids, sink_ref):
    n_kv = len(keys[0])
    groups_per_kv = q_chunks[0].shape[1] // LANES // n_kv
    in_a = lax.broadcasted_iota(jnp.int32, (CHUNK, LANES), 1) < HEAD
    n_keys = WINDOW + CHUNK
    width = keys[0][0].shape[0]
    col = lax.broadcasted_iota(jnp.int32, (1, width), 1)
    real_key = col < n_keys
    real_row = lax.broadcasted_iota(jnp.int32, (width, 1), 0) < n_keys
    ones = jnp.ones((width, LANES), BF16)
    inst = [(c, hk) for c in range(len(q_chunks)) for hk in range(n_kv)]

    fill = []
    for hk in range(n_kv):
        blocks = [jnp.full((CHUNK, width), sink_ref[hk * 2 * groups_per_kv + j], F32)
                  for j in range(2 * groups_per_kv)]
        fill.append(jnp.where(col == n_keys, jnp.concatenate(blocks, axis=0), NEG_INF))

    def scores(c, hk):
        blocks = []
        for m in range(groups_per_kv):
            lo = (hk * groups_per_kv + m) * LANES
            qg = q_chunks[c][:, lo:lo + LANES] * ATTN_SCALE
            zero = jnp.zeros_like(qg)
            blocks += [jnp.where(in_a, qg, zero), jnp.where(in_a, zero, qg)]
        s = _dot_nt(jnp.concatenate(blocks, axis=0), keys[c][hk])
        keep = real_key if valids[c] is None else jnp.logical_and(real_key, valids[c])
        return jnp.where(keep, s, fill[hk])

    s = [scores(c, hk) for c, hk in inst]
    p = [jnp.exp(x - jnp.max(x, axis=-1, keepdims=True)).astype(BF16) for x in s]
    v_ext = [jnp.concatenate([jnp.where(real_row, vals[c][hk], 0.0).astype(BF16), ones], axis=1)
             for c, hk in inst]
    pv = [jnp.dot(x, v, preferred_element_type=F32) for x, v in zip(p, v_ext)]
    o = [x[:, :LANES] / x[:, LANES:] for x in pv]
    rows = []
    for c in range(len(q_chunks)):
        groups = []
        for hk in range(n_kv):
            oc = o[c * n_kv + hk]
            for m in range(groups_per_kv):
                groups.append(jnp.where(in_a, oc[2 * m * CHUNK:(2 * m + 1) * CHUNK],
                                        oc[(2 * m + 1) * CHUNK:(2 * m + 2) * CHUNK]))
        rows.append(jnp.concatenate(groups, axis=1))
    return jnp.concatenate(rows, axis=0)


def _attn_ffn_kernel(sink_ref, q_ref, kc_ref, vc_ref, kp_ref, vp_ref, x_ref, wo_ref, bo_ref, gain_ref,
                     win_ref, wout_ref, gfin_ref, o_ref, *, streaming, ff_chunk):
    tm = x_ref.shape[-2]
    n_chunks = tm // CHUNK
    kvw = kc_ref.shape[-1]
    n_keys = WINDOW + CHUNK

    def dup(arr):
        in_a = lax.broadcasted_iota(jnp.int32, (arr.shape[0], LANES), 1) < HEAD
        res = []
        for j in range(kvw // LANES):
            res += list(_dup_heads(arr[:, j * LANES:(j + 1) * LANES], in_a))
        return res

    filler = jnp.zeros((CHUNK, kvw), F32)
    if streaming:
        q = q_ref[0]
        k_all = dup(jnp.concatenate([kp_ref[0], kc_ref[0], filler], axis=0))
        v_all = dup(jnp.concatenate([vp_ref[0], vc_ref[0], filler], axis=0))
        first_tile = pl.program_id(1) == 0
        key_idx = lax.broadcasted_iota(jnp.int32, (1, n_keys + CHUNK), 1)
    else:
        q = q_ref[...]

    q_chunks, keys, vals, valids = [], [], [], []
    for c in range(n_chunks):
        q_chunks.append(q[c * CHUNK:(c + 1) * CHUNK])
        valid = None
        if streaming:
            keys.append([ka[c * CHUNK:c * CHUNK + n_keys + CHUNK] for ka in k_all])
            vals.append([va[c * CHUNK:c * CHUNK + n_keys + CHUNK] for va in v_all])
            if c * CHUNK < WINDOW:
                valid = jnp.logical_or(key_idx + c * CHUNK >= WINDOW, jnp.logical_not(first_tile))
        else:
            keys.append(dup(jnp.concatenate([kp_ref[c], kc_ref[c * CHUNK:(c + 1) * CHUNK], filler], axis=0)))
            vals.append(dup(jnp.concatenate([vp_ref[c], vc_ref[c * CHUNK:(c + 1) * CHUNK], filler], axis=0)))
        valids.append(valid)
    o = _attend(q_chunks, keys, vals, valids, sink_ref)

    x = x_ref[0] if streaming else x_ref[...]
    x1 = x + _dot(o, wo_ref[...]) + bo_ref[...]
    x2 = _swiglu_residual(x1, gain_ref[...], win_ref, wout_ref, ff_chunk)
    y = _rms(x2) * gfin_ref[...]
    if streaming:
        o_ref[0] = y
    else:
        o_ref[...] = y


def _attn_ffn(q, k_new, v_new, k_prev, v_prev, x, sinks, w_o, b_o, gain, w_in, w_out, gain_fin, tm, ff_chunk):
    d = x.shape[-1]
    kvw = k_new.shape[-1]
    streaming = k_prev is None
    smem = pl.BlockSpec(memory_space=pltpu.SMEM)
    weights = (w_o, b_o.reshape(1, d), gain.reshape(1, d), w_in, w_out, gain_fin.reshape(1, d))
    if streaming:
        bsz, t_len, _ = x.shape
        assert tm % WINDOW == 0
        per_win = tm // WINDOW
        grid = (bsz, t_len // tm)
        tile = lambda w: pl.BlockSpec((1, tm, w), lambda b, t: (b, t, 0))
        prev = pl.BlockSpec((1, WINDOW, kvw), lambda b, t: (b, jnp.maximum(t * per_win - 1, 0), 0))
        const = lambda arr: _resident(arr.shape, lambda b, t: (0, 0))
        in_specs = [smem, tile(d), tile(kvw), tile(kvw), prev, prev, tile(d)] + [const(a) for a in weights]
        args = (sinks, q, k_new, v_new, k_new, v_new, x) + weights
        out_spec = tile(d)
        sem = ("arbitrary", "arbitrary")
    else:
        n = x.shape[0]
        grid = (1,)
        full = lambda arr: pl.BlockSpec(arr.shape, lambda i: (0,) * arr.ndim)
        args = (sinks, q, k_new, v_new, k_prev, v_prev, x) + weights
        in_specs = [smem] + [full(a) for a in args[1:]]
        out_spec = pl.BlockSpec((n, d), lambda i: (0, 0))
        sem = ("arbitrary",)
    return pl.pallas_call(
        functools.partial(_attn_ffn_kernel, streaming=streaming, ff_chunk=ff_chunk),
        grid=grid,
        in_specs=in_specs,
        out_specs=out_spec,
        out_shape=jax.ShapeDtypeStruct(x.shape, F32),
        compiler_params=pltpu.CompilerParams(dimension_semantics=sem, vmem_limit_bytes=VMEM_LIMIT_BYTES),
    )(*args)


def _tile(n, cap):
    t = min(n, cap)
    while n % t:
        t -= 1
    return t


def kernel(x_prompt, x_sample, state_wkv, state_shift, cache_k, cache_v, norm_mix, norm_ffn, rw_mu, rw_w_rkv, rw_w0, rw_w1, rw_w2, rw_a0, rw_a1, rw_a2, rw_g1, rw_g2, rw_k_k, rw_k_a, rw_r_k, rw_lnx_w, rw_lnx_b, rw_w_o, kv_norm, w_kv, b_kv, w_q, b_q, attn_sinks, w_o, b_o, ffn_w_in, ffn_w_out, norm_final):
    depth, d = norm_mix.shape
    assert depth == 2 and rw_mu.shape[0] == 1 and w_q.shape[0] == 1, "one RWKV layer followed by one attention layer"
    n_heads = d // HEAD
    n_ff = ffn_w_out.shape[1]
    ff_chunk = 256 if n_ff % 256 == 0 else n_ff
    bf = lambda w: w.astype(BF16)
    wrkv, w1, w2, a1, a2, g1, g2 = map(bf, (rw_w_rkv[0], rw_w1[0], rw_w2[0], rw_a1[0], rw_a2[0], rw_g1[0], rw_g2[0]))
    rw_wo, wq, wkv, wo = bf(rw_w_o[0]), bf(w_q[0]), bf(w_kv), bf(w_o[0])
    win, wout = bf(ffn_w_in), bf(ffn_w_out)
    sinks = attn_sinks[0].reshape(-1)

    def run(x, pos_base, shift0, st0, win_k, win_v):
        bsz, t_len, _ = x.shape
        n_rows = bsz * t_len
        n_sub = _tile(t_len // CHUNK, 4)
        z, g, shift, st = _rwkv_layer(
            x, shift0, st0, norm_mix[0], rw_mu[0], wrkv, rw_w0[0], w1, w2, rw_a0[0], a1, a2, g1, g2,
            rw_k_k[0], rw_k_a[0], rw_r_k[0].reshape(-1), rw_lnx_w[0], rw_lnx_b[0],
            n_seq=_tile(bsz, 4 // n_sub), n_sub=n_sub)
        flat = lambda t: t.reshape(n_rows, t.shape[-1])
        x1, q, k_new, v_new = _ffn_qkv(
            flat(z), flat(g), flat(x), rw_wo, norm_ffn[0], win[0], wout[0], norm_mix[1], kv_norm, wq, b_q[0],
            wkv, b_kv, t_len, pos_base, tm=_tile(n_rows, 512), ff_chunk=ff_chunk)
        kvw = k_new.shape[-1]
        k_new = k_new.reshape(bsz, t_len, kvw)
        v_new = v_new.reshape(bsz, t_len, kvw)
        if win_k is None:
            y = _attn_ffn(q.reshape(bsz, t_len, d), k_new, v_new, None, None, x1.reshape(bsz, t_len, d), sinks,
                          wo, b_o[0], norm_ffn[1], win[1], wout[1], norm_final, tm=_tile(t_len, 512),
                          ff_chunk=ff_chunk)
            k_state, v_state = k_new[:, -WINDOW:], v_new[:, -WINDOW:]
        else:
            assert t_len == CHUNK
            kp = win_k.reshape(bsz, WINDOW, kvw)
            vp = win_v.reshape(bsz, WINDOW, kvw)
            y = _attn_ffn(q, k_new.reshape(n_rows, kvw), v_new.reshape(n_rows, kvw), kp, vp, x1, sinks,
                          wo, b_o[0], norm_ffn[1], win[1], wout[1], norm_final, tm=n_rows, ff_chunk=ff_chunk)
            k_state = jnp.concatenate([kp, k_new], axis=1)[:, -WINDOW:]
            v_state = jnp.concatenate([vp, v_new], axis=1)[:, -WINDOW:]
        n_kv = kvw // HEAD
        st = st.reshape(bsz, HEAD, n_heads, HEAD).transpose(0, 2, 3, 1)[None]
        return (y.reshape(bsz, t_len, d), st, shift.reshape(1, bsz, d),
                k_state.reshape(bsz, WINDOW, n_kv, HEAD), v_state.reshape(bsz, WINDOW, n_kv, HEAD))

    bp = x_prompt.shape[0]
    zeros_state = jnp.zeros((bp, HEAD, d), F32)
    y_p, wkv_p, shift_p, k_p, v_p = run(x_prompt, 0, jnp.zeros((bp, d), F32), zeros_state, None, None)
    st_s = state_wkv[0].transpose(0, 3, 1, 2).reshape(x_sample.shape[0], HEAD, d)
    y_s, wkv_s, shift_s, k_s, v_s = run(x_sample, PAST_LEN, state_shift[0], st_s, cache_k, cache_v)
    return (y_p, y_s, wkv_p, shift_p, k_p, v_p, wkv_s, shift_s, k_s, v_s)
```

```python
import functools

import jax
import jax.numpy as jnp
from jax import lax
from jax.experimental import pallas as pl
from jax.experimental.pallas import tpu as pltpu

F32 = jnp.float32
BF16 = jnp.bfloat16

HEAD = 64
LANES = 128
SUBLANES = 8
CHUNK = 64
WINDOW = 128
ROT_DIM = HEAD // 4
ROPE_THETA = 500000.0
RMS_EPS = 1e-5
LNX_EPS = 64e-5
NEG_INF = -1e30
ATTN_SCALE = 1.0 / (HEAD ** 0.5)
VMEM_LIMIT_BYTES = 56 * 1024 * 1024
PAST_LEN = 1024
RWKV_CHUNKS_PER_STEP = 8
ROW_TILE = 512
FF_SLICE = 256


def _resident(shape, index_map):
    return pl.BlockSpec(shape, index_map, pipeline_mode=pl.Buffered(1))


def _dot(a, b):
    return jnp.dot(a.astype(BF16), b.astype(BF16), preferred_element_type=F32)


def _dot_nt(a, b):
    return lax.dot_general(a.astype(BF16), b.astype(BF16), (((1,), (1,)), ((), ())),
                           preferred_element_type=F32)


def _dot_tn(a, b):
    return lax.dot_general(a.astype(BF16), b.astype(BF16), (((0,), (0,)), ((), ())),
                           preferred_element_type=F32)


def _split2(x):
    hi = x.astype(BF16)
    return hi, (x - hi.astype(F32)).astype(BF16)


def _rms(x):
    return x * lax.rsqrt(jnp.mean(x * x, axis=-1, keepdims=True) + RMS_EPS)


def _sigmoid(x):
    return 1.0 / (1.0 + jnp.exp(-x))


def _head_ones2():
    r = lax.broadcasted_iota(jnp.int32, (2 * LANES, LANES), 0)
    c = lax.broadcasted_iota(jnp.int32, (2 * LANES, LANES), 1)
    return jnp.where(((r % LANES) // HEAD) == (c // HEAD), 1.0, 0.0).astype(BF16)


def _head_sums(xs, ones2):
    lhs = jnp.concatenate([jnp.concatenate(list(_split2(x)), axis=1) for x in xs], axis=0)
    out = jnp.dot(lhs, ones2, preferred_element_type=F32)
    res, lo = [], 0
    for x in xs:
        res.append(out[lo:lo + x.shape[0]])
        lo += x.shape[0]
    return res


def _pair_rhs(q, in_a):
    zero = jnp.zeros_like(q)
    return jnp.concatenate([jnp.where(in_a, q, zero), jnp.where(in_a, zero, q)], axis=0).astype(BF16)


def _pairmm(p, q, in_a):
    return jnp.dot(p.astype(BF16), _pair_rhs(q, in_a), preferred_element_type=F32)


def _each(fn, *cols):
    return [fn(*xs) for xs in zip(*cols)]


def _decay_and_rate(mix, w0_ref, w1_ref, w2_ref, a0_ref, a1_ref, a2_ref):
    wl = w0_ref[...] + _dot(jnp.tanh(_dot(mix(1), w1_ref[...])), w2_ref[...])
    sp = jnp.maximum(-wl, 0.0) + jnp.log(1.0 + jnp.exp(-jnp.abs(wl)))
    logdecay = -jnp.exp(-sp - 0.5)
    a = _sigmoid(a0_ref[...] + _dot(_dot(mix(4), a1_ref[...]), a2_ref[...]))
    return logdecay, a


def _decay_factors(w_blocks):
    shape = (CHUNK, LANES)
    incl = lax.broadcasted_iota(jnp.int32, shape, 1) % HEAD <= lax.broadcasted_iota(jnp.int32, shape, 0)
    tri2 = jnp.where(incl, 1.0, 0.0).astype(BF16)
    g = _each(lambda x: jnp.dot(tri2, jnp.concatenate(list(_split2(x)), axis=0), preferred_element_type=F32),
              w_blocks)
    return _each(jnp.exp, g), _each(lambda x: jnp.exp(-x), g), _each(lambda x: jnp.exp(-x), w_blocks)


def _receptance_key_value_gate(mix, a, wrkv_ref, g1_ref, g2_ref, kk_ref, ka_ref, ones2):
    k = _dot(mix(2), wrkv_ref[1])
    r = _dot(mix(0), wrkv_ref[0])
    v = _dot(mix(3), wrkv_ref[2])
    g = _dot(_sigmoid(_dot(mix(5), g1_ref[...])), g2_ref[...])
    kk = k * kk_ref[...]
    groups = [kk[:, p * LANES:(p + 1) * LANES] for p in range(k.shape[1] // LANES)]
    ss = _head_sums([x * x for x in groups], ones2)
    kkn = jnp.concatenate([x * lax.rsqrt(jnp.maximum(s, 1e-24)) for x, s in zip(groups, ss)], axis=1)
    return r, k * (1.0 + (a - 1.0) * ka_ref[...]), v, -kkn, kkn * a, g


def _wkv_chunks(blk, items, decay_factors, st, rk_ref, lw_ref, lb_ref, ones2):
    shape = (CHUNK, LANES)
    row = lax.broadcasted_iota(jnp.int32, shape, 0)
    lane = lax.broadcasted_iota(jnp.int32, shape, 1)
    in_a = lane < HEAD
    li = lane % HEAD
    strict = li < row
    incl = li <= row
    eye2 = jnp.where(li == row, 1.0, 0.0).astype(F32)
    mm = lambda p_, q_: _pairmm(p_, q_, in_a)

    r, k, v, a, b = ([blk(name, it) for it in items] for name in "rkvab")
    e, ei, ew = decay_factors
    at = _each(lambda a_, e_, ew_: a_ * e_ * ew_, a, e, ew)
    rt = _each(lambda r_, e_: r_ * e_, r, e)
    bt = _each(lambda b_, x: b_ * x, b, ei)
    kt = _each(lambda k_, x: k_ * x, k, ei)
    e_last = _each(lambda e_: e_[CHUNK - 1:CHUNK, :], e)
    bdec = _each(lambda x, el: x * el, bt, e_last)
    kdec = _each(lambda x, el: x * el, kt, e_last)

    def pair_products(at_, rt_, bt_, kt_):
        zero = jnp.zeros_like(bt_)
        rhs4 = jnp.concatenate([jnp.where(in_a, bt_, zero), jnp.where(in_a, zero, bt_),
                                jnp.where(in_a, kt_, zero), jnp.where(in_a, zero, kt_)], axis=0)
        return _dot_nt(jnp.concatenate([at_, rt_], axis=0), rhs4)

    aa = _each(pair_products, at, rt, bt, kt)
    a_ab = _each(lambda x: jnp.where(strict, x[0:CHUNK, 0:LANES], 0.0), aa)
    a_ak = _each(lambda x: jnp.where(strict, x[0:CHUNK, LANES:2 * LANES], 0.0), aa)
    a_rb = _each(lambda x: jnp.where(incl, x[CHUNK:2 * CHUNK, 0:LANES], 0.0), aa)
    a_rk = _each(lambda x: jnp.where(incl, x[CHUNK:2 * CHUNK, LANES:2 * LANES], 0.0), aa)

    def square_and_extend(xp, ti):
        res = jnp.dot(jnp.concatenate([xp, ti], axis=0).astype(BF16), _pair_rhs(xp, in_a),
                      preferred_element_type=F32)
        return res[0:CHUNK], ti + res[CHUNK:2 * CHUNK]

    xpow = _each(mm, a_ab, a_ab)
    tinv = _each(lambda x: eye2 + x, a_ab)
    for _ in range(4):
        both = _each(square_and_extend, xpow, tinv)
        xpow = [x for x, _ in both]
        tinv = [t_ for _, t_ in both]
    tinv = _each(lambda t_, x: t_ + mm(t_, x), tinv, xpow)

    at2 = _each(mm, tinv, at)
    wv = _each(mm, tinv, _each(mm, a_ak, v))
    pn = _each(lambda bd, kd, at2_, wv_, v_: _dot_tn(
        jnp.concatenate([bd, kd], axis=0),
        jnp.concatenate([jnp.concatenate([at2_, wv_], axis=1),
                         jnp.concatenate([jnp.zeros_like(v_), v_], axis=1)], axis=0)), bdec, kdec, at2, wv, v)
    pick = lambda x: jnp.where(in_a, x[0:CHUNK], x[CHUNK:2 * CHUNK])
    p_mat = _each(lambda x: pick(x[:, 0:LANES]), pn)
    n_mat = _each(lambda x: pick(x[:, LANES:2 * LANES]), pn)
    q_mat = _each(lambda rt_, arb, at2_: rt_ + mm(arb, at2_), rt, a_rb, at2)
    y0 = _each(lambda arb, ark, wv_, v_: jnp.dot(
        jnp.concatenate([arb, ark], axis=1).astype(BF16),
        jnp.concatenate([_pair_rhs(wv_, in_a), _pair_rhs(v_, in_a)], axis=0), preferred_element_type=F32),
        a_rb, a_rk, wv, v)
    lanes_of = lambda it: slice(it[1] * LANES, (it[1] + 1) * LANES)
    rk = [rk_ref[:, lanes_of(it)] for it in items]
    sums = _head_sums(_each(lambda el: eye2 * el, e_last) + _each(lambda r_, k_, rk_: r_ * k_ * rk_, r, k, rk),
                      ones2)
    decay, bonus = sums[:len(items)], sums[len(items):]

    y = []
    for i, it in enumerate(items):
        key = it[:2]
        res = jnp.dot(jnp.concatenate([q_mat[i], p_mat[i]], axis=0).astype(BF16), _pair_rhs(st[key], in_a),
                      preferred_element_type=F32)
        y.append(res[0:CHUNK] + y0[i])
        st[key] = decay[i] * st[key] + res[CHUNK:2 * CHUNK] + n_mat[i]

    yc = _each(lambda y_, m_: y_ - m_ * (1.0 / HEAD), y, _head_sums(y, ones2))
    var = _each(lambda x: x * (1.0 / HEAD), _head_sums(_each(lambda x: x * x, yc), ones2))
    return [yc_ * lax.rsqrt(var_ + LNX_EPS) * lw_ref[:, lanes_of(it)] + lb_ref[:, lanes_of(it)] + bonus_ * v_
            for yc_, var_, bonus_, v_, it in zip(yc, var, bonus, v, items)]


def _rwkv_kernel(x_ref, sh0_ref, s0_ref, gain_ref, mu_ref, wrkv_ref, w0_ref, w1_ref, w2_ref, a0_ref, a1_ref, a2_ref,
                 g1_ref, g2_ref, kk_ref, ka_ref, rk_ref, lw_ref, lb_ref,
                 z_ref, g_out, sh_out, s_out, carry_ref, st_ref):
    t = pl.program_id(1)
    n_seq, rows, d = x_ref.shape
    n_sub = rows // CHUNK
    n_pp = d // LANES

    @pl.when(t == 0)
    def _():
        carry_ref[...] = sh0_ref[...]
        st_ref[...] = s0_ref[...]

    x = x_ref[...].reshape(n_seq * rows, d)
    h = _rms(x) * gain_ref[...]
    rolled = pltpu.roll(h, 1, 0)
    first = lax.broadcasted_iota(jnp.int32, (SUBLANES, 1), 0) == 0
    pieces = []
    for q in range(n_seq):
        base = q * rows
        pieces += [jnp.where(first, carry_ref[q], rolled[base:base + SUBLANES]),
                   rolled[base + SUBLANES:base + rows]]
        last = h[base + rows - 1:base + rows, :]
        carry_ref[q] = last
        sh_out[q] = last
    prev = jnp.concatenate(pieces, axis=0)

    ones2 = _head_ones2()
    items = [(q, p, s) for s in range(n_sub) for q in range(n_seq) for p in range(n_pp)]
    vals = {}

    def blk(name, it):
        q, p, s = it
        lo = q * rows + s * CHUNK
        return vals[name][lo:lo + CHUNK, p * LANES:(p + 1) * LANES]

    xx = prev - h
    mix = lambda i: h + xx * mu_ref[i:i + 1, :]
    vals["w"], a_rate = _decay_and_rate(mix, w0_ref, w1_ref, w2_ref, a0_ref, a1_ref, a2_ref)
    vals["r"], vals["k"], vals["v"], vals["a"], vals["b"], g = _receptance_key_value_gate(
        mix, a_rate, wrkv_ref, g1_ref, g2_ref, kk_ref, ka_ref, ones2)
    g_out[...] = g.reshape(n_seq, rows, d)
    decay_factors = _decay_factors([blk("w", it) for it in items])

    st = {(q, p): st_ref[q, :, p * LANES:(p + 1) * LANES] for q in range(n_seq) for p in range(n_pp)}
    z = _wkv_chunks(blk, items, decay_factors, st, rk_ref, lw_ref, lb_ref, ones2)
    for z_, (q, p, s) in zip(z, items):
        z_ref[q, s * CHUNK:(s + 1) * CHUNK, p * LANES:(p + 1) * LANES] = z_
    for (q, p), val in st.items():
        st_ref[q, :, p * LANES:(p + 1) * LANES] = val

    @pl.when(t == pl.num_programs(1) - 1)
    def _():
        for (q, p), val in st.items():
            s_out[q, :, p * LANES:(p + 1) * LANES] = val


def _rwkv_layer(x, shift0, st0, gain, mu, wrkv, w0, w1, w2, a0, a1, a2, g1, g2, k_k, k_a, r_k, lnx_w, lnx_b,
                n_seq, n_sub):
    bsz, t_len, d = x.shape
    rows = n_sub * CHUNK
    row = lambda arr: arr.reshape(1, d)
    const = lambda arr: _resident(arr.shape, lambda b, t: (0,) * arr.ndim)
    seq = pl.BlockSpec((n_seq, rows, d), lambda b, t: (b, t, 0))
    per_seq = lambda n: pl.BlockSpec((n_seq, n, d), lambda b, t: (b, 0, 0))
    args = (x, shift0.reshape(bsz, 1, d), st0, row(gain), mu, wrkv, row(w0), w1, w2, row(a0), a1, a2, g1, g2,
            row(k_k), row(k_a), row(r_k), row(lnx_w), row(lnx_b))
    act = jax.ShapeDtypeStruct((bsz, t_len, d), F32)
    return pl.pallas_call(
        _rwkv_kernel,
        grid=(bsz // n_seq, t_len // rows),
        in_specs=[seq, per_seq(1), per_seq(HEAD)] + [const(a) for a in args[3:]],
        out_specs=[seq, seq, per_seq(1), per_seq(HEAD)],
        out_shape=[act, act, jax.ShapeDtypeStruct((bsz, 1, d), F32), jax.ShapeDtypeStruct((bsz, HEAD, d), F32)],
        scratch_shapes=[pltpu.VMEM((n_seq, 1, d), F32), pltpu.VMEM((n_seq, HEAD, d), F32)],
        compiler_params=pltpu.CompilerParams(dimension_semantics=("arbitrary", "arbitrary"),
                                             vmem_limit_bytes=VMEM_LIMIT_BYTES),
    )(*args)


def _swiglu_residual(x, gain, win_ref, wout_ref, ff_chunk):
    n_ff = wout_ref.shape[0]
    hn = (_rms(x) * gain).astype(BF16)
    acc = x
    for j in range(n_ff // ff_chunk):
        lo = j * ff_chunk
        gate = jnp.dot(hn, win_ref[:, lo:lo + ff_chunk], preferred_element_type=F32)
        up = jnp.dot(hn, win_ref[:, n_ff + lo:n_ff + lo + ff_chunk], preferred_element_type=F32)
        hid = gate * _sigmoid(gate) * up
        acc = acc + jnp.dot(hid.astype(BF16), wout_ref[lo:lo + ff_chunk, :], preferred_element_type=F32)
    return acc


def _rope(x, cs, sn, first):
    out = []
    for p in range(x.shape[1] // LANES):
        xp = x[:, p * LANES:(p + 1) * LANES]
        partner = jnp.where(first, pltpu.roll(xp, LANES - ROT_DIM // 2, 1), pltpu.roll(xp, ROT_DIM // 2, 1))
        out.append(xp * cs + partner * sn)
    return jnp.concatenate(out, axis=1)


def _ffn_qkv_kernel(z_ref, g_ref, x_ref, wo_ref, gain_ref, win_ref, wout_ref, gq_ref, gkv_ref, wq_ref, bq_ref,
                    wkv_ref, bkv_ref, inv_ref, x_out, q_out, k_out, v_out, *tails, ff_chunk, seq_len, pos_base):
    x1 = x_ref[...] + _dot(z_ref[...] * g_ref[...], wo_ref[...])
    x2 = _swiglu_residual(x1, gain_ref[...], win_ref, wout_ref, ff_chunk)
    x_out[...] = x2

    tm = x2.shape[0]
    kvw = k_out.shape[1]
    y = _rms(x2)
    q = _dot(y * gq_ref[...], wq_ref[...]) + bq_ref[...]
    kv = _dot(y * gkv_ref[...], wkv_ref[...]) + bkv_ref[...]
    rows = lax.broadcasted_iota(jnp.int32, (tm, 1), 0) + pl.program_id(0) * tm
    pos = (lax.rem(rows, seq_len) + pos_base).astype(F32)
    ang = pos * inv_ref[...]
    lane = lax.broadcasted_iota(jnp.int32, (1, LANES), 1) % HEAD
    first = lane < ROT_DIM // 2
    cs = jnp.cos(ang)
    sn = jnp.where(first, -1.0, 1.0) * jnp.sin(ang)
    q_out[...] = _rope(q, cs, sn, first)
    k = _rope(kv[:, :kvw], cs, sn, first)
    k_out[...] = k
    v_out[...] = kv[:, kvw:]
    if tails:
        tails[0][0] = k[tm - WINDOW:, :]
        tails[1][0] = kv[tm - WINDOW:, kvw:]


def _ffn_qkv(z, g, x, w_o, gain, w_in, w_out, gain_q, gain_kv, w_q, b_q, w_kv, b_kv, seq_len, pos_base, tm,
             ff_chunk):
    n, d = x.shape
    kvw = w_kv.shape[1] // 2
    half = ROT_DIM // 2
    lane = jnp.arange(LANES) % HEAD
    inv = jnp.power(jnp.float32(ROPE_THETA), -(lane % half).astype(F32) * (2.0 / ROT_DIM))
    inv = jnp.where(lane < ROT_DIM, inv, 0.0).reshape(1, LANES)
    tile = lambda w: pl.BlockSpec((tm, w), lambda i: (i, 0))
    const = lambda arr: _resident(arr.shape, lambda i: (0, 0))
    consts = (w_o, gain.reshape(1, d), w_in, w_out, gain_q.reshape(1, d), gain_kv.reshape(1, d), w_q,
              b_q.reshape(1, -1), w_kv, b_kv.reshape(1, -1), inv)
    qw = w_q.shape[1]
    out_specs = [tile(d), tile(qw), tile(kvw), tile(kvw)]
    out_shape = [jax.ShapeDtypeStruct((n, d), F32), jax.ShapeDtypeStruct((n, qw), F32),
                 jax.ShapeDtypeStruct((n, kvw), F32), jax.ShapeDtypeStruct((n, kvw), F32)]
    if seq_len % tm == 0 and tm >= WINDOW:
        tiles_per_seq = seq_len // tm
        tail = pl.BlockSpec((1, WINDOW, kvw), lambda i: (i // tiles_per_seq, 0, 0))
        out_specs += [tail, tail]
        out_shape += [jax.ShapeDtypeStruct((n // seq_len, WINDOW, kvw), F32)] * 2
    return pl.pallas_call(
        functools.partial(_ffn_qkv_kernel, ff_chunk=ff_chunk, seq_len=seq_len, pos_base=pos_base),
        grid=(n // tm,),
        in_specs=[tile(d)] * 3 + [const(a) for a in consts],
        out_specs=out_specs,
        out_shape=out_shape,
        compiler_params=pltpu.CompilerParams(dimension_semantics=("arbitrary",),
                                             vmem_limit_bytes=VMEM_LIMIT_BYTES),
    )(z, g, x, *consts)


def _dup_heads(kp, in_a):
    rolled = pltpu.roll(kp, HEAD, 1)
    return jnp.where(in_a, kp, rolled), jnp.where(in_a, rolled, kp)


def _attend(q_chunks, keys, vals, valids, sink_ref):
    n_kv = len(keys[0])
    groups_per_kv = q_chunks[0].shape[1] // LANES // n_kv
    in_a = lax.broadcasted_iota(jnp.int32, (CHUNK, LANES), 1) < HEAD
    n_keys = WINDOW + CHUNK
    width = keys[0][0].shape[0]
    col = lax.broadcasted_iota(jnp.int32, (1, width), 1)
    real_key = col < n_keys
    real_row = lax.broadcasted_iota(jnp.int32, (width, 1), 0) < n_keys
    ones = jnp.ones((width, LANES), BF16)
    inst = [(c, hk) for c in range(len(q_chunks)) for hk in range(n_kv)]

    fill = []
    for hk in range(n_kv):
        blocks = [jnp.full((CHUNK, width), sink_ref[hk * 2 * groups_per_kv + j], F32)
                  for j in range(2 * groups_per_kv)]
        fill.append(jnp.where(col == n_keys, jnp.concatenate(blocks, axis=0), NEG_INF))

    def scores(c, hk):
        blocks = []
        for m in range(groups_per_kv):
            lo = (hk * groups_per_kv + m) * LANES
            qg = q_chunks[c][:, lo:lo + LANES] * ATTN_SCALE
            zero = jnp.zeros_like(qg)
            blocks += [jnp.where(in_a, qg, zero), jnp.where(in_a, zero, qg)]
        s = _dot_nt(jnp.concatenate(blocks, axis=0), keys[c][hk])
        keep = real_key if valids[c] is None else jnp.logical_and(real_key, valids[c])
        return jnp.where(keep, s, fill[hk])

    s = [scores(c, hk) for c, hk in inst]
    p = [jnp.exp(x - jnp.max(x, axis=-1, keepdims=True)).astype(BF16) for x in s]
    v_ext = [jnp.concatenate([jnp.where(real_row, vals[c][hk], 0.0).astype(BF16), ones], axis=1)
             for c, hk in inst]
    pv = [jnp.dot(x, v, preferred_element_type=F32) for x, v in zip(p, v_ext)]
    o = [x[:, :LANES] / x[:, LANES:] for x in pv]
    rows = []
    for c in range(len(q_chunks)):
        groups = []
        for hk in range(n_kv):
            oc = o[c * n_kv + hk]
            for m in range(groups_per_kv):
                groups.append(jnp.where(in_a, oc[2 * m * CHUNK:(2 * m + 1) * CHUNK],
                                        oc[(2 * m + 1) * CHUNK:(2 * m + 2) * CHUNK]))
        rows.append(jnp.concatenate(groups, axis=1))
    return jnp.concatenate(rows, axis=0)


def _attn_ffn_kernel(sink_ref, q_ref, kc_ref, vc_ref, kp_ref, vp_ref, x_ref, wo_ref, bo_ref, gain_ref,
                     win_ref, wout_ref, gfin_ref, o_ref, *, streaming, ff_chunk):
    tm = x_ref.shape[-2]
    n_chunks = tm // CHUNK
    kvw = kc_ref.shape[-1]
    n_keys = WINDOW + CHUNK

    def dup(arr):
        in_a = lax.broadcasted_iota(jnp.int32, (arr.shape[0], LANES), 1) < HEAD
        res = []
        for j in range(kvw // LANES):
            res += list(_dup_heads(arr[:, j * LANES:(j + 1) * LANES], in_a))
        return res

    filler = jnp.zeros((CHUNK, kvw), F32)
    if streaming:
        q = q_ref[0]
        k_all = dup(jnp.concatenate([kp_ref[0], kc_ref[0], filler], axis=0))
        v_all = dup(jnp.concatenate([vp_ref[0], vc_ref[0], filler], axis=0))
        first_tile = pl.program_id(1) == 0
        key_idx = lax.broadcasted_iota(jnp.int32, (1, n_keys + CHUNK), 1)
    else:
        q = q_ref[...]

    q_chunks, keys, vals, valids = [], [], [], []
    for c in range(n_chunks):
        q_chunks.append(q[c * CHUNK:(c + 1) * CHUNK])
        valid = None
        if streaming:
            keys.append([ka[c * CHUNK:c * CHUNK + n_keys + CHUNK] for ka in k_all])
            vals.append([va[c * CHUNK:c * CHUNK + n_keys + CHUNK] for va in v_all])
            if c * CHUNK < WINDOW:
                valid = jnp.logical_or(key_idx + c * CHUNK >= WINDOW, jnp.logical_not(first_tile))
        else:
            keys.append(dup(jnp.concatenate([kp_ref[c], kc_ref[c * CHUNK:(c + 1) * CHUNK], filler], axis=0)))
            vals.append(dup(jnp.concatenate([vp_ref[c], vc_ref[c * CHUNK:(c + 1) * CHUNK], filler], axis=0)))
        valids.append(valid)
    o = _attend(q_chunks, keys, vals, valids, sink_ref)

    x = x_ref[0] if streaming else x_ref[...]
    x1 = x + _dot(o, wo_ref[...]) + bo_ref[...]
    x2 = _swiglu_residual(x1, gain_ref[...], win_ref, wout_ref, ff_chunk)
    y = _rms(x2) * gfin_ref[...]
    if streaming:
        o_ref[0] = y
    else:
        o_ref[...] = y


def _attn_ffn(q, k_new, v_new, k_prev, v_prev, x, sinks, w_o, b_o, gain, w_in, w_out, gain_fin, tm, ff_chunk):
    d = x.shape[-1]
    kvw = k_new.shape[-1]
    streaming = k_prev is None
    smem = pl.BlockSpec(memory_space=pltpu.SMEM)
    weights = (w_o, b_o.reshape(1, d), gain.reshape(1, d), w_in, w_out, gain_fin.reshape(1, d))
    if streaming:
        bsz, t_len, _ = x.shape
        assert tm % WINDOW == 0
        per_win = tm // WINDOW
        grid = (bsz, t_len // tm)
        tile = lambda w: pl.BlockSpec((1, tm, w), lambda b, t: (b, t, 0))
        prev = pl.BlockSpec((1, WINDOW, kvw), lambda b, t: (b, jnp.maximum(t * per_win - 1, 0), 0))
        const = lambda arr: _resident(arr.shape, lambda b, t: (0, 0))
        in_specs = [smem, tile(d), tile(kvw), tile(kvw), prev, prev, tile(d)] + [const(a) for a in weights]
        args = (sinks, q, k_new, v_new, k_new, v_new, x) + weights
        out_spec = tile(d)
        sem = ("arbitrary", "arbitrary")
    else:
        n = x.shape[0]
        grid = (1,)
        full = lambda arr: pl.BlockSpec(arr.shape, lambda i: (0,) * arr.ndim)
        args = (sinks, q, k_new, v_new, k_prev, v_prev, x) + weights
        in_specs = [smem] + [full(a) for a in args[1:]]
        out_spec = pl.BlockSpec((n, d), lambda i: (0, 0))
        sem = ("arbitrary",)
    return pl.pallas_call(
        functools.partial(_attn_ffn_kernel, streaming=streaming, ff_chunk=ff_chunk),
        grid=grid,
        in_specs=in_specs,
        out_specs=out_spec,
        out_shape=jax.ShapeDtypeStruct(x.shape, F32),
        compiler_params=pltpu.CompilerParams(dimension_semantics=sem, vmem_limit_bytes=VMEM_LIMIT_BYTES),
    )(*args)


def _tile(n, cap):
    t = min(n, cap)
    while n % t:
        t -= 1
    return t


def kernel(x_prompt, x_sample, state_wkv, state_shift, cache_k, cache_v, norm_mix, norm_ffn, rw_mu, rw_w_rkv, rw_w0, rw_w1, rw_w2, rw_a0, rw_a1, rw_a2, rw_g1, rw_g2, rw_k_k, rw_k_a, rw_r_k, rw_lnx_w, rw_lnx_b, rw_w_o, kv_norm, w_kv, b_kv, w_q, b_q, attn_sinks, w_o, b_o, ffn_w_in, ffn_w_out, norm_final):
    depth, d = norm_mix.shape
    assert depth == 2 and rw_mu.shape[0] == 1 and w_q.shape[0] == 1, "one RWKV layer followed by one attention layer"
    n_heads = d // HEAD
    n_ff = ffn_w_out.shape[1]
    ff_chunk = FF_SLICE if n_ff % FF_SLICE == 0 else n_ff
    bf = lambda w: w.astype(BF16)
    wrkv, w1, w2, a1, a2, g1, g2 = map(bf, (rw_w_rkv[0], rw_w1[0], rw_w2[0], rw_a1[0], rw_a2[0], rw_g1[0], rw_g2[0]))
    rw_wo, wq, wkv, wo = bf(rw_w_o[0]), bf(w_q[0]), bf(w_kv), bf(w_o[0])
    win, wout = bf(ffn_w_in), bf(ffn_w_out)
    sinks = attn_sinks[0].reshape(-1)

    def run(x, pos_base, shift0, st0, win_k, win_v):
        bsz, t_len, _ = x.shape
        n_rows = bsz * t_len
        n_sub = _tile(t_len // CHUNK, RWKV_CHUNKS_PER_STEP)
        z, g, shift, st = _rwkv_layer(
            x, shift0, st0, norm_mix[0], rw_mu[0], wrkv, rw_w0[0], w1, w2, rw_a0[0], a1, a2, g1, g2,
            rw_k_k[0], rw_k_a[0], rw_r_k[0].reshape(-1), rw_lnx_w[0], rw_lnx_b[0],
            n_seq=_tile(bsz, RWKV_CHUNKS_PER_STEP // n_sub), n_sub=n_sub)
        flat = lambda t: t.reshape(n_rows, t.shape[-1])
        x1, q, k_new, v_new, *tails = _ffn_qkv(
            flat(z), flat(g), flat(x), rw_wo, norm_ffn[0], win[0], wout[0], norm_mix[1], kv_norm, wq, b_q[0],
            wkv, b_kv, t_len, pos_base, tm=_tile(n_rows, ROW_TILE), ff_chunk=ff_chunk)
        kvw = k_new.shape[-1]
        k_new = k_new.reshape(bsz, t_len, kvw)
        v_new = v_new.reshape(bsz, t_len, kvw)
        if win_k is None:
            y = _attn_ffn(q.reshape(bsz, t_len, d), k_new, v_new, None, None, x1.reshape(bsz, t_len, d), sinks,
                          wo, b_o[0], norm_ffn[1], win[1], wout[1], norm_final, tm=_tile(t_len, ROW_TILE),
                          ff_chunk=ff_chunk)
            k_state, v_state = tails if tails else (k_new[:, -WINDOW:], v_new[:, -WINDOW:])
        else:
            assert t_len == CHUNK
            kp = win_k.reshape(bsz, WINDOW, kvw)
            vp = win_v.reshape(bsz, WINDOW, kvw)
            y = _attn_ffn(q, k_new.reshape(n_rows, kvw), v_new.reshape(n_rows, kvw), kp, vp, x1, sinks,
                          wo, b_o[0], norm_ffn[1], win[1], wout[1], norm_final, tm=n_rows, ff_chunk=ff_chunk)
            k_state = jnp.concatenate([kp, k_new], axis=1)[:, -WINDOW:]
            v_state = jnp.concatenate([vp, v_new], axis=1)[:, -WINDOW:]
        n_kv = kvw // HEAD
        st = st.reshape(bsz, HEAD, n_heads, HEAD).transpose(0, 2, 3, 1)[None]
        return (y.reshape(bsz, t_len, d), st, shift.reshape(1, bsz, d),
                k_state.reshape(bsz, WINDOW, n_kv, HEAD), v_state.reshape(bsz, WINDOW, n_kv, HEAD))

    bp = x_prompt.shape[0]
    zeros_state = jnp.zeros((bp, HEAD, d), F32)
    y_p, wkv_p, shift_p, k_p, v_p = run(x_prompt, 0, jnp.zeros((bp, d), F32), zeros_state, None, None)
    st_s = state_wkv[0].transpose(0, 3, 1, 2).reshape(x_sample.shape[0], HEAD, d)
    y_s, wkv_s, shift_s, k_s, v_s = run(x_sample, PAST_LEN, state_shift[0], st_s, cache_k, cache_v)
    return (y_p, y_s, wkv_p, shift_p, k_p, v_p, wkv_s, shift_s, k_s, v_s)
```

```python
import functools

import jax
import jax.numpy as jnp
from jax import lax
from jax.experimental import pallas as pl
from jax.experimental.pallas import tpu as pltpu

F32 = jnp.float32
BF16 = jnp.bfloat16

HEAD = 64
LANES = 128
SUBLANES = 8
CHUNK = 64
WINDOW = 128
ROT_DIM = HEAD // 4
ROPE_THETA = 500000.0
RMS_EPS = 1e-5
LNX_EPS = 64e-5
NEG_INF = -1e30
ATTN_SCALE = 1.0 / (HEAD ** 0.5)
VMEM_LIMIT_BYTES = 56 * 1024 * 1024
PAST_LEN = 1024
RWKV_CHUNKS_PER_STEP = 8
ROW_TILE = 512
FF_SLICE = 256


def _resident(shape, index_map):
    return pl.BlockSpec(shape, index_map, pipeline_mode=pl.Buffered(1))


def _dot(a, b):
    return jnp.dot(a.astype(BF16), b.astype(BF16), preferred_element_type=F32)


def _dot_nt(a, b):
    return lax.dot_general(a.astype(BF16), b.astype(BF16), (((1,), (1,)), ((), ())),
                           preferred_element_type=F32)


def _dot_tn(a, b):
    return lax.dot_general(a.astype(BF16), b.astype(BF16), (((0,), (0,)), ((), ())),
                           preferred_element_type=F32)


def _split2(x):
    hi = x.astype(BF16)
    return hi, (x - hi.astype(F32)).astype(BF16)


def _rms(x):
    return x * lax.rsqrt(jnp.mean(x * x, axis=-1, keepdims=True) + RMS_EPS)


def _sigmoid(x):
    return 1.0 / (1.0 + jnp.exp(-x))


def _head_ones2():
    r = lax.broadcasted_iota(jnp.int32, (2 * LANES, LANES), 0)
    c = lax.broadcasted_iota(jnp.int32, (2 * LANES, LANES), 1)
    return jnp.where(((r % LANES) // HEAD) == (c // HEAD), 1.0, 0.0).astype(BF16)


def _head_sums(xs, ones2):
    lhs = jnp.concatenate([jnp.concatenate(list(_split2(x)), axis=1) for x in xs], axis=0)
    out = jnp.dot(lhs, ones2, preferred_element_type=F32)
    res, lo = [], 0
    for x in xs:
        res.append(out[lo:lo + x.shape[0]])
        lo += x.shape[0]
    return res


def _pair_rhs(q, in_a):
    zero = jnp.zeros_like(q)
    return jnp.concatenate([jnp.where(in_a, q, zero), jnp.where(in_a, zero, q)], axis=0).astype(BF16)


def _pairmm(p, q, in_a):
    return jnp.dot(p.astype(BF16), _pair_rhs(q, in_a), preferred_element_type=F32)


def _each(fn, *cols):
    return [fn(*xs) for xs in zip(*cols)]


def _decay_and_rate(mix, w0_ref, w1_ref, w2_ref, a0_ref, a1_ref, a2_ref):
    wl = w0_ref[...] + _dot(jnp.tanh(_dot(mix(1), w1_ref[...])), w2_ref[...])
    sp = jnp.maximum(-wl, 0.0) + jnp.log(1.0 + jnp.exp(-jnp.abs(wl)))
    logdecay = -jnp.exp(-sp - 0.5)
    a = _sigmoid(a0_ref[...] + _dot(_dot(mix(4), a1_ref[...]), a2_ref[...]))
    return logdecay, a


def _decay_factors(w_blocks):
    shape = (CHUNK, LANES)
    incl = lax.broadcasted_iota(jnp.int32, shape, 1) % HEAD <= lax.broadcasted_iota(jnp.int32, shape, 0)
    tri2 = jnp.where(incl, 1.0, 0.0).astype(BF16)
    g = _each(lambda x: jnp.dot(tri2, jnp.concatenate(list(_split2(x)), axis=0), preferred_element_type=F32),
              w_blocks)
    return _each(jnp.exp, g), _each(lambda x: jnp.exp(-x), g), _each(lambda x: jnp.exp(-x), w_blocks)


def _receptance_key_value_gate(mix, a, wrkv_ref, g1_ref, g2_ref, kk_ref, ka_ref, ones2):
    k = _dot(mix(2), wrkv_ref[1])
    r = _dot(mix(0), wrkv_ref[0])
    v = _dot(mix(3), wrkv_ref[2])
    g = _dot(_sigmoid(_dot(mix(5), g1_ref[...])), g2_ref[...])
    kk = k * kk_ref[...]
    groups = [kk[:, p * LANES:(p + 1) * LANES] for p in range(k.shape[1] // LANES)]
    ss = _head_sums([x * x for x in groups], ones2)
    kkn = jnp.concatenate([x * lax.rsqrt(jnp.maximum(s, 1e-24)) for x, s in zip(groups, ss)], axis=1)
    return r, k * (1.0 + (a - 1.0) * ka_ref[...]), v, -kkn, kkn * a, g


def _wkv_chunks(blk, items, decay_factors, st, rk_ref, lw_ref, lb_ref, ones2):
    shape = (CHUNK, LANES)
    row = lax.broadcasted_iota(jnp.int32, shape, 0)
    lane = lax.broadcasted_iota(jnp.int32, shape, 1)
    in_a = lane < HEAD
    li = lane % HEAD
    strict = li < row
    incl = li <= row
    eye2 = jnp.where(li == row, 1.0, 0.0).astype(F32)
    mm = lambda p_, q_: _pairmm(p_, q_, in_a)

    r, k, v, a, b = ([blk(name, it) for it in items] for name in "rkvab")
    e, ei, ew = decay_factors
    at = _each(lambda a_, e_, ew_: a_ * e_ * ew_, a, e, ew)
    rt = _each(lambda r_, e_: r_ * e_, r, e)
    bt = _each(lambda b_, x: b_ * x, b, ei)
    kt = _each(lambda k_, x: k_ * x, k, ei)
    e_last = _each(lambda e_: e_[CHUNK - 1:CHUNK, :], e)
    bdec = _each(lambda x, el: x * el, bt, e_last)
    kdec = _each(lambda x, el: x * el, kt, e_last)

    def pair_products(at_, rt_, bt_, kt_):
        zero = jnp.zeros_like(bt_)
        rhs4 = jnp.concatenate([jnp.where(in_a, bt_, zero), jnp.where(in_a, zero, bt_),
                                jnp.where(in_a, kt_, zero), jnp.where(in_a, zero, kt_)], axis=0)
        return _dot_nt(jnp.concatenate([at_, rt_], axis=0), rhs4)

    aa = _each(pair_products, at, rt, bt, kt)
    a_ab = _each(lambda x: jnp.where(strict, x[0:CHUNK, 0:LANES], 0.0), aa)
    a_ak = _each(lambda x: jnp.where(strict, x[0:CHUNK, LANES:2 * LANES], 0.0), aa)
    a_rb = _each(lambda x: jnp.where(incl, x[CHUNK:2 * CHUNK, 0:LANES], 0.0), aa)
    a_rk = _each(lambda x: jnp.where(incl, x[CHUNK:2 * CHUNK, LANES:2 * LANES], 0.0), aa)

    def square_and_extend(xp, ti):
        res = jnp.dot(jnp.concatenate([xp, ti], axis=0).astype(BF16), _pair_rhs(xp, in_a),
                      preferred_element_type=F32)
        return res[0:CHUNK], ti + res[CHUNK:2 * CHUNK]

    xpow = _each(mm, a_ab, a_ab)
    tinv = _each(lambda x: eye2 + x, a_ab)
    for _ in range(4):
        both = _each(square_and_extend, xpow, tinv)
        xpow = [x for x, _ in both]
        tinv = [t_ for _, t_ in both]
    tinv = _each(lambda t_, x: t_ + mm(t_, x), tinv, xpow)

    at2 = _each(mm, tinv, at)
    wv = _each(mm, tinv, _each(mm, a_ak, v))
    pn = _each(lambda bd, kd, at2_, wv_, v_: _dot_tn(
        jnp.concatenate([bd, kd], axis=0),
        jnp.concatenate([jnp.concatenate([at2_, wv_], axis=1),
                         jnp.concatenate([jnp.zeros_like(v_), v_], axis=1)], axis=0)), bdec, kdec, at2, wv, v)
    pick = lambda x: jnp.where(in_a, x[0:CHUNK], x[CHUNK:2 * CHUNK])
    p_mat = _each(lambda x: pick(x[:, 0:LANES]), pn)
    n_mat = _each(lambda x: pick(x[:, LANES:2 * LANES]), pn)
    q_mat = _each(lambda rt_, arb, at2_: rt_ + mm(arb, at2_), rt, a_rb, at2)
    y0 = _each(lambda arb, ark, wv_, v_: jnp.dot(
        jnp.concatenate([arb, ark], axis=1).astype(BF16),
        jnp.concatenate([_pair_rhs(wv_, in_a), _pair_rhs(v_, in_a)], axis=0), preferred_element_type=F32),
        a_rb, a_rk, wv, v)
    lanes_of = lambda it: slice(it[1] * LANES, (it[1] + 1) * LANES)
    rk = [rk_ref[:, lanes_of(it)] for it in items]
    sums = _head_sums(_each(lambda el: eye2 * el, e_last) + _each(lambda r_, k_, rk_: r_ * k_ * rk_, r, k, rk),
                      ones2)
    decay, bonus = sums[:len(items)], sums[len(items):]

    y = []
    for i, it in enumerate(items):
        key = it[:2]
        res = jnp.dot(jnp.concatenate([q_mat[i], p_mat[i]], axis=0).astype(BF16), _pair_rhs(st[key], in_a),
                      preferred_element_type=F32)
        y.append(res[0:CHUNK] + y0[i])
        st[key] = decay[i] * st[key] + res[CHUNK:2 * CHUNK] + n_mat[i]

    yc = _each(lambda y_, m_: y_ - m_ * (1.0 / HEAD), y, _head_sums(y, ones2))
    var = _each(lambda x: x * (1.0 / HEAD), _head_sums(_each(lambda x: x * x, yc), ones2))
    return [yc_ * lax.rsqrt(var_ + LNX_EPS) * lw_ref[:, lanes_of(it)] + lb_ref[:, lanes_of(it)] + bonus_ * v_
            for yc_, var_, bonus_, v_, it in zip(yc, var, bonus, v, items)]


def _rwkv_kernel(x_ref, sh0_ref, s0_ref, gain_ref, mu_ref, wrkv_ref, w0_ref, w1_ref, w2_ref, a0_ref, a1_ref, a2_ref,
                 g1_ref, g2_ref, kk_ref, ka_ref, rk_ref, lw_ref, lb_ref,
                 z_ref, g_out, sh_out, s_out, carry_ref, st_ref):
    t = pl.program_id(1)
    n_seq, rows, d = x_ref.shape
    n_sub = rows // CHUNK
    n_pp = d // LANES

    @pl.when(t == 0)
    def _():
        carry_ref[...] = sh0_ref[...]
        st_ref[...] = s0_ref[...]

    x = x_ref[...].reshape(n_seq * rows, d)
    h = _rms(x) * gain_ref[...]
    rolled = pltpu.roll(h, 1, 0)
    first = lax.broadcasted_iota(jnp.int32, (SUBLANES, 1), 0) == 0
    pieces = []
    for q in range(n_seq):
        base = q * rows
        pieces += [jnp.where(first, carry_ref[q], rolled[base:base + SUBLANES]),
                   rolled[base + SUBLANES:base + rows]]
        last = h[base + rows - 1:base + rows, :]
        carry_ref[q] = last
        sh_out[q] = last
    prev = jnp.concatenate(pieces, axis=0)

    ones2 = _head_ones2()
    items = [(q, p, s) for s in range(n_sub) for q in range(n_seq) for p in range(n_pp)]
    vals = {}

    def blk(name, it):
        q, p, s = it
        lo = q * rows + s * CHUNK
        return vals[name][lo:lo + CHUNK, p * LANES:(p + 1) * LANES]

    xx = prev - h
    mix = lambda i: h + xx * mu_ref[i:i + 1, :]
    vals["w"], a_rate = _decay_and_rate(mix, w0_ref, w1_ref, w2_ref, a0_ref, a1_ref, a2_ref)
    vals["r"], vals["k"], vals["v"], vals["a"], vals["b"], g = _receptance_key_value_gate(
        mix, a_rate, wrkv_ref, g1_ref, g2_ref, kk_ref, ka_ref, ones2)
    g_out[...] = g.reshape(n_seq, rows, d)
    decay_factors = _decay_factors([blk("w", it) for it in items])

    st = {(q, p): st_ref[q, :, p * LANES:(p + 1) * LANES] for q in range(n_seq) for p in range(n_pp)}
    z = _wkv_chunks(blk, items, decay_factors, st, rk_ref, lw_ref, lb_ref, ones2)
    for z_, (q, p, s) in zip(z, items):
        z_ref[q, s * CHUNK:(s + 1) * CHUNK, p * LANES:(p + 1) * LANES] = z_
    for (q, p), val in st.items():
        st_ref[q, :, p * LANES:(p + 1) * LANES] = val

    @pl.when(t == pl.num_programs(1) - 1)
    def _():
        for (q, p), val in st.items():
            s_out[q, :, p * LANES:(p + 1) * LANES] = val


def _rwkv_layer(x, shift0, st0, gain, mu, wrkv, w0, w1, w2, a0, a1, a2, g1, g2, k_k, k_a, r_k, lnx_w, lnx_b,
                n_seq, n_sub):
    bsz, t_len, d = x.shape
    rows = n_sub * CHUNK
    row = lambda arr: arr.reshape(1, d)
    const = lambda arr: _resident(arr.shape, lambda b, t: (0,) * arr.ndim)
    seq = pl.BlockSpec((n_seq, rows, d), lambda b, t: (b, t, 0))
    per_seq = lambda n: pl.BlockSpec((n_seq, n, d), lambda b, t: (b, 0, 0))
    args = (x, shift0.reshape(bsz, 1, d), st0, row(gain), mu, wrkv, row(w0), w1, w2, row(a0), a1, a2, g1, g2,
            row(k_k), row(k_a), row(r_k), row(lnx_w), row(lnx_b))
    act = jax.ShapeDtypeStruct((bsz, t_len, d), F32)
    return pl.pallas_call(
        _rwkv_kernel,
        grid=(bsz // n_seq, t_len // rows),
        in_specs=[seq, per_seq(1), per_seq(HEAD)] + [const(a) for a in args[3:]],
        out_specs=[seq, seq, per_seq(1), per_seq(HEAD)],
        out_shape=[act, act, jax.ShapeDtypeStruct((bsz, 1, d), F32), jax.ShapeDtypeStruct((bsz, HEAD, d), F32)],
        scratch_shapes=[pltpu.VMEM((n_seq, 1, d), F32), pltpu.VMEM((n_seq, HEAD, d), F32)],
        compiler_params=pltpu.CompilerParams(dimension_semantics=("arbitrary", "arbitrary"),
                                             vmem_limit_bytes=VMEM_LIMIT_BYTES),
    )(*args)


def _swiglu_residual(x, gain, win_ref, wout_ref, ff_chunk):
    n_ff = wout_ref.shape[0]
    hn = (_rms(x) * gain).astype(BF16)
    acc = x
    for j in range(n_ff // ff_chunk):
        lo = j * ff_chunk
        gate = jnp.dot(hn, win_ref[:, lo:lo + ff_chunk], preferred_element_type=F32)
        up = jnp.dot(hn, win_ref[:, n_ff + lo:n_ff + lo + ff_chunk], preferred_element_type=F32)
        hid = gate * _sigmoid(gate) * up
        acc = acc + jnp.dot(hid.astype(BF16), wout_ref[lo:lo + ff_chunk, :], preferred_element_type=F32)
    return acc


def _rope(x, cs, sn, first):
    out = []
    for p in range(x.shape[1] // LANES):
        xp = x[:, p * LANES:(p + 1) * LANES]
        partner = jnp.where(first, pltpu.roll(xp, LANES - ROT_DIM // 2, 1), pltpu.roll(xp, ROT_DIM // 2, 1))
        out.append(xp * cs + partner * sn)
    return jnp.concatenate(out, axis=1)


def _ffn_qkv_kernel(z_ref, g_ref, x_ref, wo_ref, gain_ref, win_ref, wout_ref, gq_ref, gkv_ref, wq_ref, bq_ref,
                    wkv_ref, bkv_ref, inv_ref, x_out, q_out, k_out, v_out, *tails, ff_chunk, seq_len, pos_base):
    x1 = x_ref[...] + _dot(z_ref[...] * g_ref[...], wo_ref[...])
    x2 = _swiglu_residual(x1, gain_ref[...], win_ref, wout_ref, ff_chunk)
    x_out[...] = x2

    tm = x2.shape[0]
    kvw = k_out.shape[1]
    y = _rms(x2)
    q = _dot(y * gq_ref[...], wq_ref[...]) + bq_ref[...]
    kv = _dot(y * gkv_ref[...], wkv_ref[...]) + bkv_ref[...]
    rows = lax.broadcasted_iota(jnp.int32, (tm, 1), 0) + pl.program_id(0) * tm
    pos = (lax.rem(rows, seq_len) + pos_base).astype(F32)
    ang = pos * inv_ref[...]
    lane = lax.broadcasted_iota(jnp.int32, (1, LANES), 1) % HEAD
    first = lane < ROT_DIM // 2
    cs = jnp.cos(ang)
    sn = jnp.where(first, -1.0, 1.0) * jnp.sin(ang)
    q_out[...] = _rope(q, cs, sn, first)
    k = _rope(kv[:, :kvw], cs, sn, first)
    k_out[...] = k
    v_out[...] = kv[:, kvw:]
    if tails:
        tails[0][0] = k[tm - WINDOW:, :]
        tails[1][0] = kv[tm - WINDOW:, kvw:]


def _ffn_qkv(z, g, x, w_o, gain, w_in, w_out, gain_q, gain_kv, w_q, b_q, w_kv, b_kv, seq_len, pos_base, tm,
             ff_chunk):
    n, d = x.shape
    kvw = w_kv.shape[1] // 2
    half = ROT_DIM // 2
    lane = jnp.arange(LANES) % HEAD
    inv = jnp.power(jnp.float32(ROPE_THETA), -(lane % half).astype(F32) * (2.0 / ROT_DIM))
    inv = jnp.where(lane < ROT_DIM, inv, 0.0).reshape(1, LANES)
    tile = lambda w: pl.BlockSpec((tm, w), lambda i: (i, 0))
    const = lambda arr: _resident(arr.shape, lambda i: (0, 0))
    consts = (w_o, gain.reshape(1, d), w_in, w_out, gain_q.reshape(1, d), gain_kv.reshape(1, d), w_q,
              b_q.reshape(1, -1), w_kv, b_kv.reshape(1, -1), inv)
    qw = w_q.shape[1]
    out_specs = [tile(d), tile(qw), tile(kvw), tile(kvw)]
    out_shape = [jax.ShapeDtypeStruct((n, d), F32), jax.ShapeDtypeStruct((n, qw), F32),
                 jax.ShapeDtypeStruct((n, kvw), F32), jax.ShapeDtypeStruct((n, kvw), F32)]
    if seq_len % tm == 0 and tm >= WINDOW:
        tiles_per_seq = seq_len // tm
        tail = pl.BlockSpec((1, WINDOW, kvw), lambda i: (i // tiles_per_seq, 0, 0))
        out_specs += [tail, tail]
        out_shape += [jax.ShapeDtypeStruct((n // seq_len, WINDOW, kvw), F32)] * 2
    return pl.pallas_call(
        functools.partial(_ffn_qkv_kernel, ff_chunk=ff_chunk, seq_len=seq_len, pos_base=pos_base),
        grid=(n // tm,),
        in_specs=[tile(d)] * 3 + [const(a) for a in consts],
        out_specs=out_specs,
        out_shape=out_shape,
        compiler_params=pltpu.CompilerParams(dimension_semantics=("arbitrary",),
                                             vmem_limit_bytes=VMEM_LIMIT_BYTES),
    )(z, g, x, *consts)


def _dup_heads(kp, in_a):
    rolled = pltpu.roll(kp, HEAD, 1)
    return jnp.where(in_a, kp, rolled), jnp.where(in_a, rolled, kp)


def _attend(q_chunks, keys, vals, valids, sink_ref):
    n_kv = len(keys[0])
    groups_per_kv = q_chunks[0].shape[1] // LANES // n_kv
    in_a = lax.broadcasted_iota(jnp.int32, (CHUNK, LANES), 1) < HEAD
    n_keys = WINDOW + CHUNK
    width = keys[0][0].shape[0]
    col = lax.broadcasted_iota(jnp.int32, (1, width), 1)
    real_key = col < n_keys
    real_row = lax.broadcasted_iota(jnp.int32, (width, 1), 0) < n_keys
    ones = jnp.ones((width, LANES), BF16)
    inst = [(c, hk) for c in range(len(q_chunks)) for hk in range(n_kv)]

    fill = []
    for hk in range(n_kv):
        blocks = [jnp.full((CHUNK, width), sink_ref[hk * 2 * groups_per_kv + j], F32)
                  for j in range(2 * groups_per_kv)]
        fill.append(jnp.where(col == n_keys, jnp.concatenate(blocks, axis=0), NEG_INF))

    def scores(c, hk):
        blocks = []
        for m in range(groups_per_kv):
            lo = (hk * groups_per_kv + m) * LANES
            qg = q_chunks[c][:, lo:lo + LANES] * ATTN_SCALE
            zero = jnp.zeros_like(qg)
            blocks += [jnp.where(in_a, qg, zero), jnp.where(in_a, zero, qg)]
        s = _dot_nt(jnp.concatenate(blocks, axis=0), keys[c][hk])
        keep = real_key if valids[c] is None else jnp.logical_and(real_key, valids[c])
        return jnp.where(keep, s, fill[hk])

    s = [scores(c, hk) for c, hk in inst]
    p = [jnp.exp(x - jnp.max(x, axis=-1, keepdims=True)).astype(BF16) for x in s]
    v_ext = [jnp.concatenate([jnp.where(real_row, vals[c][hk], 0.0).astype(BF16), ones], axis=1)
             for c, hk in inst]
    pv = [jnp.dot(x, v, preferred_element_type=F32) for x, v in zip(p, v_ext)]
    o = [x[:, :LANES] / x[:, LANES:] for x in pv]
    rows = []
    for c in range(len(q_chunks)):
        groups = []
        for hk in range(n_kv):
            oc = o[c * n_kv + hk]
            for m in range(groups_per_kv):
                groups.append(jnp.where(in_a, oc[2 * m * CHUNK:(2 * m + 1) * CHUNK],
                                        oc[(2 * m + 1) * CHUNK:(2 * m + 2) * CHUNK]))
        rows.append(jnp.concatenate(groups, axis=1))
    return jnp.concatenate(rows, axis=0)


def _attn_ffn_kernel(sink_ref, q_ref, kc_ref, vc_ref, kp_ref, vp_ref, x_ref, wo_ref, bo_ref, gain_ref,
                     win_ref, wout_ref, gfin_ref, o_ref, *, streaming, ff_chunk):
    tm = x_ref.shape[-2]
    n_chunks = tm // CHUNK
    kvw = kc_ref.shape[-1]
    n_keys = WINDOW + CHUNK

    def dup(arr):
        in_a = lax.broadcasted_iota(jnp.int32, (arr.shape[0], LANES), 1) < HEAD
        res = []
        for j in range(kvw // LANES):
            res += list(_dup_heads(arr[:, j * LANES:(j + 1) * LANES], in_a))
        return res

    filler = jnp.zeros((CHUNK, kvw), F32)
    if streaming:
        q = q_ref[0]
        k_all = dup(jnp.concatenate([kp_ref[0], kc_ref[0], filler], axis=0))
        v_all = dup(jnp.concatenate([vp_ref[0], vc_ref[0], filler], axis=0))
        first_tile = pl.program_id(1) == 0
        key_idx = lax.broadcasted_iota(jnp.int32, (1, n_keys + CHUNK), 1)
    else:
        q = q_ref[...]

    q_chunks, keys, vals, valids = [], [], [], []
    for c in range(n_chunks):
        q_chunks.append(q[c * CHUNK:(c + 1) * CHUNK])
        valid = None
        if streaming:
            keys.append([ka[c * CHUNK:c * CHUNK + n_keys + CHUNK] for ka in k_all])
            vals.append([va[c * CHUNK:c * CHUNK + n_keys + CHUNK] for va in v_all])
            if c * CHUNK < WINDOW:
                valid = jnp.logical_or(key_idx + c * CHUNK >= WINDOW, jnp.logical_not(first_tile))
        else:
            keys.append(dup(jnp.concatenate([kp_ref[c], kc_ref[c * CHUNK:(c + 1) * CHUNK], filler], axis=0)))
            vals.append(dup(jnp.concatenate([vp_ref[c], vc_ref[c * CHUNK:(c + 1) * CHUNK], filler], axis=0)))
        valids.append(valid)
    o = _attend(q_chunks, keys, vals, valids, sink_ref)

    x = x_ref[0] if streaming else x_ref[...]
    x1 = x + _dot(o, wo_ref[...]) + bo_ref[...]
    x2 = _swiglu_residual(x1, gain_ref[...], win_ref, wout_ref, ff_chunk)
    y = _rms(x2) * gfin_ref[...]
    if streaming:
        o_ref[0] = y
    else:
        o_ref[...] = y


def _attn_ffn(q, k_new, v_new, k_prev, v_prev, x, sinks, w_o, b_o, gain, w_in, w_out, gain_fin, tm, ff_chunk):
    d = x.shape[-1]
    kvw = k_new.shape[-1]
    streaming = k_prev is None
    smem = pl.BlockSpec(memory_space=pltpu.SMEM)
    weights = (w_o, b_o.reshape(1, d), gain.reshape(1, d), w_in, w_out, gain_fin.reshape(1, d))
    if streaming:
        bsz, t_len, _ = x.shape
        assert tm % WINDOW == 0
        per_win = tm // WINDOW
        grid = (bsz, t_len // tm)
        tile = lambda w: pl.BlockSpec((1, tm, w), lambda b, t: (b, t, 0))
        prev = pl.BlockSpec((1, WINDOW, kvw), lambda b, t: (b, jnp.maximum(t * per_win - 1, 0), 0))
        const = lambda arr: _resident(arr.shape, lambda b, t: (0, 0))
        in_specs = [smem, tile(d), tile(kvw), tile(kvw), prev, prev, tile(d)] + [const(a) for a in weights]
        args = (sinks, q, k_new, v_new, k_new, v_new, x) + weights
        out_spec = tile(d)
        sem = ("arbitrary", "arbitrary")
    else:
        n = x.shape[0]
        grid = (1,)
        full = lambda arr: pl.BlockSpec(arr.shape, lambda i: (0,) * arr.ndim)
        args = (sinks, q, k_new, v_new, k_prev, v_prev, x) + weights
        in_specs = [smem] + [full(a) for a in args[1:]]
        out_spec = pl.BlockSpec((n, d), lambda i: (0, 0))
        sem = ("arbitrary",)
    return pl.pallas_call(
        functools.partial(_attn_ffn_kernel, streaming=streaming, ff_chunk=ff_chunk),
        grid=grid,
        in_specs=in_specs,
        out_specs=out_spec,
        out_shape=jax.ShapeDtypeStruct(x.shape, F32),
        compiler_params=pltpu.CompilerParams(dimension_semantics=sem, vmem_limit_bytes=VMEM_LIMIT_BYTES),
    )(*args)


def _tile(n, cap):
    t = min(n, cap)
    while n % t:
        t -= 1
    return t


def kernel(x_prompt, x_sample, state_wkv, state_shift, cache_k, cache_v, norm_mix, norm_ffn, rw_mu, rw_w_rkv, rw_w0, rw_w1, rw_w2, rw_a0, rw_a1, rw_a2, rw_g1, rw_g2, rw_k_k, rw_k_a, rw_r_k, rw_lnx_w, rw_lnx_b, rw_w_o, kv_norm, w_kv, b_kv, w_q, b_q, attn_sinks, w_o, b_o, ffn_w_in, ffn_w_out, norm_final):
    depth, d = norm_mix.shape
    assert depth == 2 and rw_mu.shape[0] == 1 and w_q.shape[0] == 1, "one RWKV layer followed by one attention layer"
    n_heads = d // HEAD
    n_ff = ffn_w_out.shape[1]
    ff_chunk = FF_SLICE if n_ff % FF_SLICE == 0 else n_ff
    bf = lambda w: w.astype(BF16)
    wrkv, w1, w2, a1, a2, g1, g2 = map(bf, (rw_w_rkv[0], rw_w1[0], rw_w2[0], rw_a1[0], rw_a2[0], rw_g1[0], rw_g2[0]))
    rw_wo, wq, wkv, wo = bf(rw_w_o[0]), bf(w_q[0]), bf(w_kv), bf(w_o[0])
    win = [bf(ffn_w_in[l]) for l in range(depth)]
    wout = [bf(ffn_w_out[l]) for l in range(depth)]
    sinks = attn_sinks[0].reshape(-1)

    def run(x, pos_base, shift0, st0, win_k, win_v):
        bsz, t_len, _ = x.shape
        n_rows = bsz * t_len
        n_sub = _tile(t_len // CHUNK, RWKV_CHUNKS_PER_STEP)
        z, g, shift, st = _rwkv_layer(
            x, shift0, st0, norm_mix[0], rw_mu[0], wrkv, rw_w0[0], w1, w2, rw_a0[0], a1, a2, g1, g2,
            rw_k_k[0], rw_k_a[0], rw_r_k[0].reshape(-1), rw_lnx_w[0], rw_lnx_b[0],
            n_seq=_tile(bsz, RWKV_CHUNKS_PER_STEP // n_sub), n_sub=n_sub)
        flat = lambda t: t.reshape(n_rows, t.shape[-1])
        x1, q, k_new, v_new, *tails = _ffn_qkv(
            flat(z), flat(g), flat(x), rw_wo, norm_ffn[0], win[0], wout[0], norm_mix[1], kv_norm, wq, b_q[0],
            wkv, b_kv, t_len, pos_base, tm=_tile(n_rows, ROW_TILE), ff_chunk=ff_chunk)
        kvw = k_new.shape[-1]
        k_new = k_new.reshape(bsz, t_len, kvw)
        v_new = v_new.reshape(bsz, t_len, kvw)
        if win_k is None:
            y = _attn_ffn(q.reshape(bsz, t_len, d), k_new, v_new, None, None, x1.reshape(bsz, t_len, d), sinks,
                          wo, b_o[0], norm_ffn[1], win[1], wout[1], norm_final, tm=_tile(t_len, ROW_TILE),
                          ff_chunk=ff_chunk)
            k_state, v_state = tails if tails else (k_new[:, -WINDOW:], v_new[:, -WINDOW:])
        else:
            assert t_len == CHUNK
            kp = win_k.reshape(bsz, WINDOW, kvw)
            vp = win_v.reshape(bsz, WINDOW, kvw)
            y = _attn_ffn(q, k_new.reshape(n_rows, kvw), v_new.reshape(n_rows, kvw), kp, vp, x1, sinks,
                          wo, b_o[0], norm_ffn[1], win[1], wout[1], norm_final, tm=n_rows, ff_chunk=ff_chunk)
            k_state = jnp.concatenate([kp, k_new], axis=1)[:, -WINDOW:]
            v_state = jnp.concatenate([vp, v_new], axis=1)[:, -WINDOW:]
        n_kv = kvw // HEAD
        st = st.reshape(bsz, HEAD, n_heads, HEAD).transpose(0, 2, 3, 1)[None]
        return (y.reshape(bsz, t_len, d), st, shift.reshape(1, bsz, d),
                k_state.reshape(bsz, WINDOW, n_kv, HEAD), v_state.reshape(bsz, WINDOW, n_kv, HEAD))

    bp = x_prompt.shape[0]
    zeros_state = jnp.zeros((bp, HEAD, d), F32)
    y_p, wkv_p, shift_p, k_p, v_p = run(x_prompt, 0, jnp.zeros((bp, d), F32), zeros_state, None, None)
    st_s = state_wkv[0].transpose(0, 3, 1, 2).reshape(x_sample.shape[0], HEAD, d)
    y_s, wkv_s, shift_s, k_s, v_s = run(x_sample, PAST_LEN, state_shift[0], st_s, cache_k, cache_v)
    return (y_p, y_s, wkv_p, shift_p, k_p, v_p, wkv_s, shift_s, k_s, v_s)
```

```python
import functools

import jax
import jax.numpy as jnp
from jax import lax
from jax.experimental import pallas as pl
from jax.experimental.pallas import tpu as pltpu

F32 = jnp.float32
BF16 = jnp.bfloat16

HEAD = 64
LANES = 128
SUBLANES = 8
CHUNK = 64
WINDOW = 128
ROT_DIM = HEAD // 4
ROPE_THETA = 500000.0
RMS_EPS = 1e-5
LNX_EPS = 64e-5
NEG_INF = -1e30
ATTN_SCALE = 1.0 / (HEAD ** 0.5)
VMEM_LIMIT_BYTES = 56 * 1024 * 1024
PAST_LEN = 1024
RWKV_CHUNKS_PER_STEP = 8
ROW_TILE = 512
FF_SLICE = 256


def _resident(shape, index_map):
    return pl.BlockSpec(shape, index_map, pipeline_mode=pl.Buffered(1))


def _weight_spec(arr, layer):
    if arr.ndim == 2:
        return _resident(arr.shape, lambda *_: (0, 0))
    return _resident((None,) + arr.shape[1:], lambda *_: (layer, 0, 0))


def _dot(a, b):
    return jnp.dot(a.astype(BF16), b.astype(BF16), preferred_element_type=F32)


def _dot_nt(a, b):
    return lax.dot_general(a.astype(BF16), b.astype(BF16), (((1,), (1,)), ((), ())),
                           preferred_element_type=F32)


def _dot_tn(a, b):
    return lax.dot_general(a.astype(BF16), b.astype(BF16), (((0,), (0,)), ((), ())),
                           preferred_element_type=F32)


def _split2(x):
    hi = x.astype(BF16)
    return hi, (x - hi.astype(F32)).astype(BF16)


def _rms(x):
    return x * lax.rsqrt(jnp.mean(x * x, axis=-1, keepdims=True) + RMS_EPS)


def _sigmoid(x):
    return 1.0 / (1.0 + jnp.exp(-x))


def _head_ones2():
    r = lax.broadcasted_iota(jnp.int32, (2 * LANES, LANES), 0)
    c = lax.broadcasted_iota(jnp.int32, (2 * LANES, LANES), 1)
    return jnp.where(((r % LANES) // HEAD) == (c // HEAD), 1.0, 0.0).astype(BF16)


def _head_sums(xs, ones2):
    lhs = jnp.concatenate([jnp.concatenate(list(_split2(x)), axis=1) for x in xs], axis=0)
    out = jnp.dot(lhs, ones2, preferred_element_type=F32)
    res, lo = [], 0
    for x in xs:
        res.append(out[lo:lo + x.shape[0]])
        lo += x.shape[0]
    return res


def _pair_rhs(q, in_a):
    zero = jnp.zeros_like(q)
    return jnp.concatenate([jnp.where(in_a, q, zero), jnp.where(in_a, zero, q)], axis=0).astype(BF16)


def _pairmm(p, q, in_a):
    return jnp.dot(p.astype(BF16), _pair_rhs(q, in_a), preferred_element_type=F32)


def _each(fn, *cols):
    return [fn(*xs) for xs in zip(*cols)]


def _decay_and_rate(mix, w0_ref, w1_ref, w2_ref, a0_ref, a1_ref, a2_ref):
    wl = w0_ref[...] + _dot(jnp.tanh(_dot(mix(1), w1_ref[...])), w2_ref[...])
    sp = jnp.maximum(-wl, 0.0) + jnp.log(1.0 + jnp.exp(-jnp.abs(wl)))
    logdecay = -jnp.exp(-sp - 0.5)
    a = _sigmoid(a0_ref[...] + _dot(_dot(mix(4), a1_ref[...]), a2_ref[...]))
    return logdecay, a


def _decay_factors(w_blocks):
    shape = (CHUNK, LANES)
    incl = lax.broadcasted_iota(jnp.int32, shape, 1) % HEAD <= lax.broadcasted_iota(jnp.int32, shape, 0)
    tri2 = jnp.where(incl, 1.0, 0.0).astype(BF16)
    g = _each(lambda x: jnp.dot(tri2, jnp.concatenate(list(_split2(x)), axis=0), preferred_element_type=F32),
              w_blocks)
    return _each(jnp.exp, g), _each(lambda x: jnp.exp(-x), g), _each(lambda x: jnp.exp(-x), w_blocks)


def _receptance_key_value_gate(mix, a, wrkv_ref, g1_ref, g2_ref, kk_ref, ka_ref, ones2):
    k = _dot(mix(2), wrkv_ref[1])
    r = _dot(mix(0), wrkv_ref[0])
    v = _dot(mix(3), wrkv_ref[2])
    g = _dot(_sigmoid(_dot(mix(5), g1_ref[...])), g2_ref[...])
    kk = k * kk_ref[...]
    groups = [kk[:, p * LANES:(p + 1) * LANES] for p in range(k.shape[1] // LANES)]
    ss = _head_sums([x * x for x in groups], ones2)
    kkn = jnp.concatenate([x * lax.rsqrt(jnp.maximum(s, 1e-24)) for x, s in zip(groups, ss)], axis=1)
    return r, k * (1.0 + (a - 1.0) * ka_ref[...]), v, -kkn, kkn * a, g


def _wkv_chunks(blk, items, decay_factors, st, rk_ref, lw_ref, lb_ref, ones2):
    shape = (CHUNK, LANES)
    row = lax.broadcasted_iota(jnp.int32, shape, 0)
    lane = lax.broadcasted_iota(jnp.int32, shape, 1)
    in_a = lane < HEAD
    li = lane % HEAD
    strict = li < row
    incl = li <= row
    eye2 = jnp.where(li == row, 1.0, 0.0).astype(F32)
    mm = lambda p_, q_: _pairmm(p_, q_, in_a)

    r, k, v, a, b = ([blk(name, it) for it in items] for name in "rkvab")
    e, ei, ew = decay_factors
    at = _each(lambda a_, e_, ew_: a_ * e_ * ew_, a, e, ew)
    rt = _each(lambda r_, e_: r_ * e_, r, e)
    bt = _each(lambda b_, x: b_ * x, b, ei)
    kt = _each(lambda k_, x: k_ * x, k, ei)
    e_last = _each(lambda e_: e_[CHUNK - 1:CHUNK, :], e)
    bdec = _each(lambda x, el: x * el, bt, e_last)
    kdec = _each(lambda x, el: x * el, kt, e_last)

    def pair_products(at_, rt_, bt_, kt_):
        zero = jnp.zeros_like(bt_)
        rhs4 = jnp.concatenate([jnp.where(in_a, bt_, zero), jnp.where(in_a, zero, bt_),
                                jnp.where(in_a, kt_, zero), jnp.where(in_a, zero, kt_)], axis=0)
        return _dot_nt(jnp.concatenate([at_, rt_], axis=0), rhs4)

    aa = _each(pair_products, at, rt, bt, kt)
    a_ab = _each(lambda x: jnp.where(strict, x[0:CHUNK, 0:LANES], 0.0), aa)
    a_ak = _each(lambda x: jnp.where(strict, x[0:CHUNK, LANES:2 * LANES], 0.0), aa)
    a_rb = _each(lambda x: jnp.where(incl, x[CHUNK:2 * CHUNK, 0:LANES], 0.0), aa)
    a_rk = _each(lambda x: jnp.where(incl, x[CHUNK:2 * CHUNK, LANES:2 * LANES], 0.0), aa)

    def square_and_extend(xp, ti):
        res = jnp.dot(jnp.concatenate([xp, ti], axis=0).astype(BF16), _pair_rhs(xp, in_a),
                      preferred_element_type=F32)
        return res[0:CHUNK], ti + res[CHUNK:2 * CHUNK]

    xpow = _each(mm, a_ab, a_ab)
    tinv = _each(lambda x: eye2 + x, a_ab)
    for _ in range(4):
        both = _each(square_and_extend, xpow, tinv)
        xpow = [x for x, _ in both]
        tinv = [t_ for _, t_ in both]
    tinv = _each(lambda t_, x: t_ + mm(t_, x), tinv, xpow)

    at2 = _each(mm, tinv, at)
    wv = _each(mm, tinv, _each(mm, a_ak, v))
    pn = _each(lambda bd, kd, at2_, wv_, v_: _dot_tn(
        jnp.concatenate([bd, kd], axis=0),
        jnp.concatenate([jnp.concatenate([at2_, wv_], axis=1),
                         jnp.concatenate([jnp.zeros_like(v_), v_], axis=1)], axis=0)), bdec, kdec, at2, wv, v)
    pick = lambda x: jnp.where(in_a, x[0:CHUNK], x[CHUNK:2 * CHUNK])
    p_mat = _each(lambda x: pick(x[:, 0:LANES]), pn)
    n_mat = _each(lambda x: pick(x[:, LANES:2 * LANES]), pn)
    q_mat = _each(lambda rt_, arb, at2_: rt_ + mm(arb, at2_), rt, a_rb, at2)
    y0 = _each(lambda arb, ark, wv_, v_: jnp.dot(
        jnp.concatenate([arb, ark], axis=1).astype(BF16),
        jnp.concatenate([_pair_rhs(wv_, in_a), _pair_rhs(v_, in_a)], axis=0), preferred_element_type=F32),
        a_rb, a_rk, wv, v)
    lanes_of = lambda it: slice(it[1] * LANES, (it[1] + 1) * LANES)
    rk = [rk_ref[:, lanes_of(it)] for it in items]
    sums = _head_sums(_each(lambda el: eye2 * el, e_last) + _each(lambda r_, k_, rk_: r_ * k_ * rk_, r, k, rk),
                      ones2)
    decay, bonus = sums[:len(items)], sums[len(items):]

    y = []
    for i, it in enumerate(items):
        key = it[:2]
        res = jnp.dot(jnp.concatenate([q_mat[i], p_mat[i]], axis=0).astype(BF16), _pair_rhs(st[key], in_a),
                      preferred_element_type=F32)
        y.append(res[0:CHUNK] + y0[i])
        st[key] = decay[i] * st[key] + res[CHUNK:2 * CHUNK] + n_mat[i]

    yc = _each(lambda y_, m_: y_ - m_ * (1.0 / HEAD), y, _head_sums(y, ones2))
    var = _each(lambda x: x * (1.0 / HEAD), _head_sums(_each(lambda x: x * x, yc), ones2))
    return [yc_ * lax.rsqrt(var_ + LNX_EPS) * lw_ref[:, lanes_of(it)] + lb_ref[:, lanes_of(it)] + bonus_ * v_
            for yc_, var_, bonus_, v_, it in zip(yc, var, bonus, v, items)]


def _rwkv_kernel(x_ref, sh0_ref, s0_ref, gain_ref, mu_ref, wrkv_ref, w0_ref, w1_ref, w2_ref, a0_ref, a1_ref, a2_ref,
                 g1_ref, g2_ref, kk_ref, ka_ref, rk_ref, lw_ref, lb_ref,
                 z_ref, g_out, sh_out, s_out, carry_ref, st_ref):
    t = pl.program_id(1)
    n_seq, rows, d = x_ref.shape
    n_sub = rows // CHUNK
    n_pp = d // LANES

    @pl.when(t == 0)
    def _():
        carry_ref[...] = sh0_ref[...]
        st_ref[...] = s0_ref[...]

    x = x_ref[...].reshape(n_seq * rows, d)
    h = _rms(x) * gain_ref[...]
    rolled = pltpu.roll(h, 1, 0)
    first = lax.broadcasted_iota(jnp.int32, (SUBLANES, 1), 0) == 0
    pieces = []
    for q in range(n_seq):
        base = q * rows
        pieces += [jnp.where(first, carry_ref[q], rolled[base:base + SUBLANES]),
                   rolled[base + SUBLANES:base + rows]]
        last = h[base + rows - 1:base + rows, :]
        carry_ref[q] = last
        sh_out[q] = last
    prev = jnp.concatenate(pieces, axis=0)

    ones2 = _head_ones2()
    items = [(q, p, s) for s in range(n_sub) for q in range(n_seq) for p in range(n_pp)]
    vals = {}

    def blk(name, it):
        q, p, s = it
        lo = q * rows + s * CHUNK
        return vals[name][lo:lo + CHUNK, p * LANES:(p + 1) * LANES]

    xx = prev - h
    mix = lambda i: h + xx * mu_ref[i:i + 1, :]
    vals["w"], a_rate = _decay_and_rate(mix, w0_ref, w1_ref, w2_ref, a0_ref, a1_ref, a2_ref)
    vals["r"], vals["k"], vals["v"], vals["a"], vals["b"], g = _receptance_key_value_gate(
        mix, a_rate, wrkv_ref, g1_ref, g2_ref, kk_ref, ka_ref, ones2)
    g_out[...] = g.reshape(n_seq, rows, d)
    decay_factors = _decay_factors([blk("w", it) for it in items])

    st = {(q, p): st_ref[q, :, p * LANES:(p + 1) * LANES] for q in range(n_seq) for p in range(n_pp)}
    z = _wkv_chunks(blk, items, decay_factors, st, rk_ref, lw_ref, lb_ref, ones2)
    for z_, (q, p, s) in zip(z, items):
        z_ref[q, s * CHUNK:(s + 1) * CHUNK, p * LANES:(p + 1) * LANES] = z_
    for (q, p), val in st.items():
        st_ref[q, :, p * LANES:(p + 1) * LANES] = val

    @pl.when(t == pl.num_programs(1) - 1)
    def _():
        for (q, p), val in st.items():
            s_out[q, :, p * LANES:(p + 1) * LANES] = val


def _rwkv_layer(x, shift0, st0, gain, mu, wrkv, w0, w1, w2, a0, a1, a2, g1, g2, k_k, k_a, r_k, lnx_w, lnx_b,
                n_seq, n_sub):
    bsz, t_len, d = x.shape
    rows = n_sub * CHUNK
    row = lambda arr: arr.reshape(1, d)
    const = lambda arr: _resident(arr.shape, lambda b, t: (0,) * arr.ndim)
    seq = pl.BlockSpec((n_seq, rows, d), lambda b, t: (b, t, 0))
    per_seq = lambda n: pl.BlockSpec((n_seq, n, d), lambda b, t: (b, 0, 0))
    args = (x, shift0.reshape(bsz, 1, d), st0, row(gain), mu, wrkv, row(w0), w1, w2, row(a0), a1, a2, g1, g2,
            row(k_k), row(k_a), row(r_k), row(lnx_w), row(lnx_b))
    act = jax.ShapeDtypeStruct((bsz, t_len, d), F32)
    return pl.pallas_call(
        _rwkv_kernel,
        grid=(bsz // n_seq, t_len // rows),
        in_specs=[seq, per_seq(1), per_seq(HEAD)] + [const(a) for a in args[3:]],
        out_specs=[seq, seq, per_seq(1), per_seq(HEAD)],
        out_shape=[act, act, jax.ShapeDtypeStruct((bsz, 1, d), F32), jax.ShapeDtypeStruct((bsz, HEAD, d), F32)],
        scratch_shapes=[pltpu.VMEM((n_seq, 1, d), F32), pltpu.VMEM((n_seq, HEAD, d), F32)],
        compiler_params=pltpu.CompilerParams(dimension_semantics=("arbitrary", "arbitrary"),
                                             vmem_limit_bytes=VMEM_LIMIT_BYTES),
    )(*args)


def _swiglu_residual(x, gain, win_ref, wout_ref, ff_chunk):
    n_ff = wout_ref.shape[0]
    hn = (_rms(x) * gain).astype(BF16)
    acc = x
    for j in range(n_ff // ff_chunk):
        lo = j * ff_chunk
        gate = jnp.dot(hn, win_ref[:, lo:lo + ff_chunk], preferred_element_type=F32)
        up = jnp.dot(hn, win_ref[:, n_ff + lo:n_ff + lo + ff_chunk], preferred_element_type=F32)
        hid = gate * _sigmoid(gate) * up
        acc = acc + jnp.dot(hid.astype(BF16), wout_ref[lo:lo + ff_chunk, :], preferred_element_type=F32)
    return acc


def _rope(x, cs, sn, first):
    out = []
    for p in range(x.shape[1] // LANES):
        xp = x[:, p * LANES:(p + 1) * LANES]
        partner = jnp.where(first, pltpu.roll(xp, LANES - ROT_DIM // 2, 1), pltpu.roll(xp, ROT_DIM // 2, 1))
        out.append(xp * cs + partner * sn)
    return jnp.concatenate(out, axis=1)


def _ffn_qkv_kernel(z_ref, g_ref, x_ref, wo_ref, gain_ref, win_ref, wout_ref, gq_ref, gkv_ref, wq_ref, bq_ref,
                    wkv_ref, bkv_ref, inv_ref, x_out, q_out, k_out, v_out, *tails, ff_chunk, seq_len, pos_base):
    x1 = x_ref[...] + _dot(z_ref[...] * g_ref[...], wo_ref[...])
    x2 = _swiglu_residual(x1, gain_ref[...], win_ref, wout_ref, ff_chunk)
    x_out[...] = x2

    tm = x2.shape[0]
    kvw = k_out.shape[1]
    y = _rms(x2)
    q = _dot(y * gq_ref[...], wq_ref[...]) + bq_ref[...]
    kv = _dot(y * gkv_ref[...], wkv_ref[...]) + bkv_ref[...]
    rows = lax.broadcasted_iota(jnp.int32, (tm, 1), 0) + pl.program_id(0) * tm
    pos = (lax.rem(rows, seq_len) + pos_base).astype(F32)
    ang = pos * inv_ref[...]
    lane = lax.broadcasted_iota(jnp.int32, (1, LANES), 1) % HEAD
    first = lane < ROT_DIM // 2
    cs = jnp.cos(ang)
    sn = jnp.where(first, -1.0, 1.0) * jnp.sin(ang)
    q_out[...] = _rope(q, cs, sn, first)
    k = _rope(kv[:, :kvw], cs, sn, first)
    k_out[...] = k
    v_out[...] = kv[:, kvw:]
    if tails:
        tails[0][0] = k[tm - WINDOW:, :]
        tails[1][0] = kv[tm - WINDOW:, kvw:]


def _ffn_qkv(z, g, x, w_o, gain, w_in, w_out, gain_q, gain_kv, w_q, b_q, w_kv, b_kv, seq_len, pos_base, tm,
             ff_chunk, layer):
    n, d = x.shape
    kvw = w_kv.shape[1] // 2
    half = ROT_DIM // 2
    lane = jnp.arange(LANES) % HEAD
    inv = jnp.power(jnp.float32(ROPE_THETA), -(lane % half).astype(F32) * (2.0 / ROT_DIM))
    inv = jnp.where(lane < ROT_DIM, inv, 0.0).reshape(1, LANES)
    tile = lambda w: pl.BlockSpec((tm, w), lambda i: (i, 0))
    const = lambda arr: _weight_spec(arr, layer)
    consts = (w_o, gain.reshape(1, d), w_in, w_out, gain_q.reshape(1, d), gain_kv.reshape(1, d), w_q,
              b_q.reshape(1, -1), w_kv, b_kv.reshape(1, -1), inv)
    qw = w_q.shape[1]
    out_specs = [tile(d), tile(qw), tile(kvw), tile(kvw)]
    out_shape = [jax.ShapeDtypeStruct((n, d), F32), jax.ShapeDtypeStruct((n, qw), F32),
                 jax.ShapeDtypeStruct((n, kvw), F32), jax.ShapeDtypeStruct((n, kvw), F32)]
    if seq_len % tm == 0 and tm >= WINDOW:
        tiles_per_seq = seq_len // tm
        tail = pl.BlockSpec((1, WINDOW, kvw), lambda i: (i // tiles_per_seq, 0, 0))
        out_specs += [tail, tail]
        out_shape += [jax.ShapeDtypeStruct((n // seq_len, WINDOW, kvw), F32)] * 2
    return pl.pallas_call(
        functools.partial(_ffn_qkv_kernel, ff_chunk=ff_chunk, seq_len=seq_len, pos_base=pos_base),
        grid=(n // tm,),
        in_specs=[tile(d)] * 3 + [const(a) for a in consts],
        out_specs=out_specs,
        out_shape=out_shape,
        compiler_params=pltpu.CompilerParams(dimension_semantics=("arbitrary",),
                                             vmem_limit_bytes=VMEM_LIMIT_BYTES),
    )(z, g, x, *consts)


def _dup_heads(kp, in_a):
    rolled = pltpu.roll(kp, HEAD, 1)
    return jnp.where(in_a, kp, rolled), jnp.where(in_a, rolled, kp)


def _attend(q_chunks, keys, vals, valids, sink_ref):
    n_kv = len(keys[0])
    groups_per_kv = q_chunks[0].shape[1] // LANES // n_kv
    in_a = lax.broadcasted_iota(jnp.int32, (CHUNK, LANES), 1) < HEAD
    n_keys = WINDOW + CHUNK
    width = keys[0][0].shape[0]
    col = lax.broadcasted_iota(jnp.int32, (1, width), 1)
    real_key = col < n_keys
    real_row = lax.broadcasted_iota(jnp.int32, (width, 1), 0) < n_keys
    ones = jnp.ones((width, LANES), BF16)
    inst = [(c, hk) for c in range(len(q_chunks)) for hk in range(n_kv)]

    fill = []
    for hk in range(n_kv):
        blocks = [jnp.full((CHUNK, width), sink_ref[hk * 2 * groups_per_kv + j], F32)
                  for j in range(2 * groups_per_kv)]
        fill.append(jnp.where(col == n_keys, jnp.concatenate(blocks, axis=0), NEG_INF))

    def scores(c, hk):
        blocks = []
        for m in range(groups_per_kv):
            lo = (hk * groups_per_kv + m) * LANES
            qg = q_chunks[c][:, lo:lo + LANES] * ATTN_SCALE
            zero = jnp.zeros_like(qg)
            blocks += [jnp.where(in_a, qg, zero), jnp.where(in_a, zero, qg)]
        s = _dot_nt(jnp.concatenate(blocks, axis=0), keys[c][hk])
        keep = real_key if valids[c] is None else jnp.logical_and(real_key, valids[c])
        return jnp.where(keep, s, fill[hk])

    s = [scores(c, hk) for c, hk in inst]
    p = [jnp.exp(x - jnp.max(x, axis=-1, keepdims=True)).astype(BF16) for x in s]
    v_ext = [jnp.concatenate([jnp.where(real_row, vals[c][hk], 0.0).astype(BF16), ones], axis=1)
             for c, hk in inst]
    pv = [jnp.dot(x, v, preferred_element_type=F32) for x, v in zip(p, v_ext)]
    o = [x[:, :LANES] / x[:, LANES:] for x in pv]
    rows = []
    for c in range(len(q_chunks)):
        groups = []
        for hk in range(n_kv):
            oc = o[c * n_kv + hk]
            for m in range(groups_per_kv):
                groups.append(jnp.where(in_a, oc[2 * m * CHUNK:(2 * m + 1) * CHUNK],
                                        oc[(2 * m + 1) * CHUNK:(2 * m + 2) * CHUNK]))
        rows.append(jnp.concatenate(groups, axis=1))
    return jnp.concatenate(rows, axis=0)


def _attn_ffn_kernel(sink_ref, q_ref, kc_ref, vc_ref, kp_ref, vp_ref, x_ref, wo_ref, bo_ref, gain_ref,
                     win_ref, wout_ref, gfin_ref, o_ref, *, streaming, ff_chunk):
    tm = x_ref.shape[-2]
    n_chunks = tm // CHUNK
    kvw = kc_ref.shape[-1]
    n_keys = WINDOW + CHUNK

    def dup(arr):
        in_a = lax.broadcasted_iota(jnp.int32, (arr.shape[0], LANES), 1) < HEAD
        res = []
        for j in range(kvw // LANES):
            res += list(_dup_heads(arr[:, j * LANES:(j + 1) * LANES], in_a))
        return res

    filler = jnp.zeros((CHUNK, kvw), F32)
    if streaming:
        q = q_ref[0]
        k_all = dup(jnp.concatenate([kp_ref[0], kc_ref[0], filler], axis=0))
        v_all = dup(jnp.concatenate([vp_ref[0], vc_ref[0], filler], axis=0))
        first_tile = pl.program_id(1) == 0
        key_idx = lax.broadcasted_iota(jnp.int32, (1, n_keys + CHUNK), 1)
    else:
        q = q_ref[...]

    q_chunks, keys, vals, valids = [], [], [], []
    for c in range(n_chunks):
        q_chunks.append(q[c * CHUNK:(c + 1) * CHUNK])
        valid = None
        if streaming:
            keys.append([ka[c * CHUNK:c * CHUNK + n_keys + CHUNK] for ka in k_all])
            vals.append([va[c * CHUNK:c * CHUNK + n_keys + CHUNK] for va in v_all])
            if c * CHUNK < WINDOW:
                valid = jnp.logical_or(key_idx + c * CHUNK >= WINDOW, jnp.logical_not(first_tile))
        else:
            keys.append(dup(jnp.concatenate([kp_ref[c], kc_ref[c * CHUNK:(c + 1) * CHUNK], filler], axis=0)))
            vals.append(dup(jnp.concatenate([vp_ref[c], vc_ref[c * CHUNK:(c + 1) * CHUNK], filler], axis=0)))
        valids.append(valid)
    o = _attend(q_chunks, keys, vals, valids, sink_ref)

    x = x_ref[0] if streaming else x_ref[...]
    x1 = x + _dot(o, wo_ref[...]) + bo_ref[...]
    x2 = _swiglu_residual(x1, gain_ref[...], win_ref, wout_ref, ff_chunk)
    y = _rms(x2) * gfin_ref[...]
    if streaming:
        o_ref[0] = y
    else:
        o_ref[...] = y


def _attn_ffn(q, k_new, v_new, k_prev, v_prev, x, sinks, w_o, b_o, gain, w_in, w_out, gain_fin, tm, ff_chunk,
              layer):
    d = x.shape[-1]
    kvw = k_new.shape[-1]
    streaming = k_prev is None
    smem = pl.BlockSpec(memory_space=pltpu.SMEM)
    weights = (w_o, b_o.reshape(1, d), gain.reshape(1, d), w_in, w_out, gain_fin.reshape(1, d))
    if streaming:
        bsz, t_len, _ = x.shape
        assert tm % WINDOW == 0
        per_win = tm // WINDOW
        grid = (bsz, t_len // tm)
        tile = lambda w: pl.BlockSpec((1, tm, w), lambda b, t: (b, t, 0))
        prev = pl.BlockSpec((1, WINDOW, kvw), lambda b, t: (b, jnp.maximum(t * per_win - 1, 0), 0))
        in_specs = ([smem, tile(d), tile(kvw), tile(kvw), prev, prev, tile(d)]
                    + [_weight_spec(a, layer) for a in weights])
        args = (sinks, q, k_new, v_new, k_new, v_new, x) + weights
        out_spec = tile(d)
        sem = ("arbitrary", "arbitrary")
    else:
        n = x.shape[0]
        grid = (1,)
        full = lambda arr: pl.BlockSpec(arr.shape, lambda i: (0,) * arr.ndim)
        args = (sinks, q, k_new, v_new, k_prev, v_prev, x) + weights
        in_specs = [smem] + [full(a) for a in args[1:7]] + [_weight_spec(a, layer) for a in weights]
        out_spec = pl.BlockSpec((n, d), lambda i: (0, 0))
        sem = ("arbitrary",)
    return pl.pallas_call(
        functools.partial(_attn_ffn_kernel, streaming=streaming, ff_chunk=ff_chunk),
        grid=grid,
        in_specs=in_specs,
        out_specs=out_spec,
        out_shape=jax.ShapeDtypeStruct(x.shape, F32),
        compiler_params=pltpu.CompilerParams(dimension_semantics=sem, vmem_limit_bytes=VMEM_LIMIT_BYTES),
    )(*args)


def _tile(n, cap):
    t = min(n, cap)
    while n % t:
        t -= 1
    return t


def kernel(x_prompt, x_sample, state_wkv, state_shift, cache_k, cache_v, norm_mix, norm_ffn, rw_mu, rw_w_rkv, rw_w0, rw_w1, rw_w2, rw_a0, rw_a1, rw_a2, rw_g1, rw_g2, rw_k_k, rw_k_a, rw_r_k, rw_lnx_w, rw_lnx_b, rw_w_o, kv_norm, w_kv, b_kv, w_q, b_q, attn_sinks, w_o, b_o, ffn_w_in, ffn_w_out, norm_final):
    depth, d = norm_mix.shape
    assert depth == 2 and rw_mu.shape[0] == 1 and w_q.shape[0] == 1, "one RWKV layer followed by one attention layer"
    n_heads = d // HEAD
    n_ff = ffn_w_out.shape[1]
    ff_chunk = FF_SLICE if n_ff % FF_SLICE == 0 else n_ff
    bf = lambda w: w.astype(BF16)
    wrkv, w1, w2, a1, a2, g1, g2 = map(bf, (rw_w_rkv[0], rw_w1[0], rw_w2[0], rw_a1[0], rw_a2[0], rw_g1[0], rw_g2[0]))
    rw_wo, wq, wkv, wo = bf(rw_w_o[0]), bf(w_q[0]), bf(w_kv), bf(w_o[0])
    win, wout = bf(ffn_w_in), bf(ffn_w_out)
    sinks = attn_sinks[0].reshape(-1)

    def run(x, pos_base, shift0, st0, win_k, win_v):
        bsz, t_len, _ = x.shape
        n_rows = bsz * t_len
        n_sub = _tile(t_len // CHUNK, RWKV_CHUNKS_PER_STEP)
        z, g, shift, st = _rwkv_layer(
            x, shift0, st0, norm_mix[0], rw_mu[0], wrkv, rw_w0[0], w1, w2, rw_a0[0], a1, a2, g1, g2,
            rw_k_k[0], rw_k_a[0], rw_r_k[0].reshape(-1), rw_lnx_w[0], rw_lnx_b[0],
            n_seq=_tile(bsz, RWKV_CHUNKS_PER_STEP // n_sub), n_sub=n_sub)
        flat = lambda t: t.reshape(n_rows, t.shape[-1])
        x1, q, k_new, v_new, *tails = _ffn_qkv(
            flat(z), flat(g), flat(x), rw_wo, norm_ffn[0], win, wout, norm_mix[1], kv_norm, wq, b_q[0],
            wkv, b_kv, t_len, pos_base, tm=_tile(n_rows, ROW_TILE), ff_chunk=ff_chunk, layer=0)
        kvw = k_new.shape[-1]
        k_new = k_new.reshape(bsz, t_len, kvw)
        v_new = v_new.reshape(bsz, t_len, kvw)
        if win_k is None:
            y = _attn_ffn(q.reshape(bsz, t_len, d), k_new, v_new, None, None, x1.reshape(bsz, t_len, d), sinks,
                          wo, b_o[0], norm_ffn[1], win, wout, norm_final, tm=_tile(t_len, ROW_TILE),
                          ff_chunk=ff_chunk, layer=1)
            k_state, v_state = tails if tails else (k_new[:, -WINDOW:], v_new[:, -WINDOW:])
        else:
            assert t_len == CHUNK
            kp = win_k.reshape(bsz, WINDOW, kvw)
            vp = win_v.reshape(bsz, WINDOW, kvw)
            y = _attn_ffn(q, k_new.reshape(n_rows, kvw), v_new.reshape(n_rows, kvw), kp, vp, x1, sinks,
                          wo, b_o[0], norm_ffn[1], win, wout, norm_final, tm=n_rows, ff_chunk=ff_chunk,
                          layer=1)
            k_state = jnp.concatenate([kp, k_new], axis=1)[:, -WINDOW:]
            v_state = jnp.concatenate([vp, v_new], axis=1)[:, -WINDOW:]
        n_kv = kvw // HEAD
        st = st.reshape(bsz, HEAD, n_heads, HEAD).transpose(0, 2, 3, 1)[None]
        return (y.reshape(bsz, t_len, d), st, shift.reshape(1, bsz, d),
                k_state.reshape(bsz, WINDOW, n_kv, HEAD), v_state.reshape(bsz, WINDOW, n_kv, HEAD))

    bp = x_prompt.shape[0]
    zeros_state = jnp.zeros((bp, HEAD, d), F32)
    y_p, wkv_p, shift_p, k_p, v_p = run(x_prompt, 0, jnp.zeros((bp, d), F32), zeros_state, None, None)
    st_s = state_wkv[0].transpose(0, 3, 1, 2).reshape(x_sample.shape[0], HEAD, d)
    y_s, wkv_s, shift_s, k_s, v_s = run(x_sample, PAST_LEN, state_shift[0], st_s, cache_k, cache_v)
    return (y_p, y_s, wkv_p, shift_p, k_p, v_p, wkv_s, shift_s, k_s, v_s)
```

```python
import functools

import jax
import jax.numpy as jnp
from jax import lax
from jax.experimental import pallas as pl
from jax.experimental.pallas import tpu as pltpu

F32 = jnp.float32
BF16 = jnp.bfloat16

HEAD = 64
LANES = 128
SUBLANES = 8
CHUNK = 64
WINDOW = 128
ROT_DIM = HEAD // 4
ROPE_THETA = 500000.0
RMS_EPS = 1e-5
LNX_EPS = 64e-5
NEG_INF = -1e30
ATTN_SCALE = 1.0 / (HEAD ** 0.5)
VMEM_LIMIT_BYTES = 56 * 1024 * 1024
PAST_LEN = 1024
RWKV_CHUNKS_PER_STEP = 8
ROW_TILE = 512
FF_SLICE = 256


def _resident(shape, index_map):
    return pl.BlockSpec(shape, index_map, pipeline_mode=pl.Buffered(1))


def _weight_spec(arr, layer):
    if arr.ndim == 2:
        return _resident(arr.shape, lambda *_: (0, 0))
    return _resident((None,) + arr.shape[1:], lambda *_: (layer, 0, 0))


def _dot(a, b):
    return jnp.dot(a.astype(BF16), b.astype(BF16), preferred_element_type=F32)


def _dot_nt(a, b):
    return lax.dot_general(a.astype(BF16), b.astype(BF16), (((1,), (1,)), ((), ())),
                           preferred_element_type=F32)


def _dot_tn(a, b):
    return lax.dot_general(a.astype(BF16), b.astype(BF16), (((0,), (0,)), ((), ())),
                           preferred_element_type=F32)


def _split2(x):
    hi = x.astype(BF16)
    return hi, (x - hi.astype(F32)).astype(BF16)


def _rms(x):
    return x * lax.rsqrt(jnp.mean(x * x, axis=-1, keepdims=True) + RMS_EPS)


def _sigmoid(x):
    return 1.0 / (1.0 + jnp.exp(-x))


def _head_ones2():
    r = lax.broadcasted_iota(jnp.int32, (2 * LANES, LANES), 0)
    c = lax.broadcasted_iota(jnp.int32, (2 * LANES, LANES), 1)
    return jnp.where(((r % LANES) // HEAD) == (c // HEAD), 1.0, 0.0).astype(BF16)


def _head_sums(xs, ones2):
    lhs = jnp.concatenate([jnp.concatenate(list(_split2(x)), axis=1) for x in xs], axis=0)
    out = jnp.dot(lhs, ones2, preferred_element_type=F32)
    res, lo = [], 0
    for x in xs:
        res.append(out[lo:lo + x.shape[0]])
        lo += x.shape[0]
    return res


def _pair_rhs(q, in_a):
    zero = jnp.zeros_like(q)
    return jnp.concatenate([jnp.where(in_a, q, zero), jnp.where(in_a, zero, q)], axis=0).astype(BF16)


def _pairmm(p, q, in_a):
    return jnp.dot(p.astype(BF16), _pair_rhs(q, in_a), preferred_element_type=F32)


def _each(fn, *cols):
    return [fn(*xs) for xs in zip(*cols)]


def _decay_and_rate(mix, w0_ref, w1_ref, w2_ref, a0_ref, a1_ref, a2_ref):
    wl = w0_ref[...] + _dot(jnp.tanh(_dot(mix(1), w1_ref[...])), w2_ref[...])
    sp = jnp.maximum(-wl, 0.0) + jnp.log(1.0 + jnp.exp(-jnp.abs(wl)))
    logdecay = -jnp.exp(-sp - 0.5)
    a = _sigmoid(a0_ref[...] + _dot(_dot(mix(4), a1_ref[...]), a2_ref[...]))
    return logdecay, a


def _decay_factors(w_blocks):
    shape = (CHUNK, LANES)
    incl = lax.broadcasted_iota(jnp.int32, shape, 1) % HEAD <= lax.broadcasted_iota(jnp.int32, shape, 0)
    tri2 = jnp.where(incl, 1.0, 0.0).astype(BF16)
    g = _each(lambda x: jnp.dot(tri2, jnp.concatenate(list(_split2(x)), axis=0), preferred_element_type=F32),
              w_blocks)
    return _each(jnp.exp, g), _each(lambda x: jnp.exp(-x), g), _each(lambda x: jnp.exp(-x), w_blocks)


def _receptance_key_value_gate(mix, a, wrkv_ref, g1_ref, g2_ref, kk_ref, ka_ref, ones2):
    k = _dot(mix(2), wrkv_ref[1])
    r = _dot(mix(0), wrkv_ref[0])
    v = _dot(mix(3), wrkv_ref[2])
    g = _dot(_sigmoid(_dot(mix(5), g1_ref[...])), g2_ref[...])
    kk = k * kk_ref[...]
    groups = [kk[:, p * LANES:(p + 1) * LANES] for p in range(k.shape[1] // LANES)]
    ss = _head_sums([x * x for x in groups], ones2)
    kkn = jnp.concatenate([x * lax.rsqrt(jnp.maximum(s, 1e-24)) for x, s in zip(groups, ss)], axis=1)
    return r, k * (1.0 + (a - 1.0) * ka_ref[...]), v, -kkn, kkn * a, g


def _wkv_chunks(blk, items, decay_factors, st, rk_ref, lw_ref, lb_ref, ones2):
    shape = (CHUNK, LANES)
    row = lax.broadcasted_iota(jnp.int32, shape, 0)
    lane = lax.broadcasted_iota(jnp.int32, shape, 1)
    in_a = lane < HEAD
    li = lane % HEAD
    strict = li < row
    incl = li <= row
    eye2 = jnp.where(li == row, 1.0, 0.0).astype(F32)
    mm = lambda p_, q_: _pairmm(p_, q_, in_a)

    r, k, v, a, b = ([blk(name, it) for it in items] for name in "rkvab")
    e, ei, ew = decay_factors
    at = _each(lambda a_, e_, ew_: a_ * e_ * ew_, a, e, ew)
    rt = _each(lambda r_, e_: r_ * e_, r, e)
    bt = _each(lambda b_, x: b_ * x, b, ei)
    kt = _each(lambda k_, x: k_ * x, k, ei)
    e_last = _each(lambda e_: e_[CHUNK - 1:CHUNK, :], e)
    bdec = _each(lambda x, el: x * el, bt, e_last)
    kdec = _each(lambda x, el: x * el, kt, e_last)

    def pair_products(at_, rt_, bt_, kt_):
        zero = jnp.zeros_like(bt_)
        rhs4 = jnp.concatenate([jnp.where(in_a, bt_, zero), jnp.where(in_a, zero, bt_),
                                jnp.where(in_a, kt_, zero), jnp.where(in_a, zero, kt_)], axis=0)
        return _dot_nt(jnp.concatenate([at_, rt_], axis=0), rhs4)

    aa = _each(pair_products, at, rt, bt, kt)
    a_ab = _each(lambda x: jnp.where(strict, x[0:CHUNK, 0:LANES], 0.0), aa)
    a_ak = _each(lambda x: jnp.where(strict, x[0:CHUNK, LANES:2 * LANES], 0.0), aa)
    a_rb = _each(lambda x: jnp.where(incl, x[CHUNK:2 * CHUNK, 0:LANES], 0.0), aa)
    a_rk = _each(lambda x: jnp.where(incl, x[CHUNK:2 * CHUNK, LANES:2 * LANES], 0.0), aa)

    tinv = _each(lambda x: eye2 + jnp.where(row // 2 == li // 2, x, 0.0), a_ab)
    half = 2
    while half < CHUNK:
        lower_left = (row // (2 * half) == li // (2 * half)) & (row // half != li // half)
        coupled = _each(mm, _each(lambda x: jnp.where(lower_left, x, 0.0), a_ab), tinv)
        tinv = _each(lambda t_, y_: t_ + mm(t_, y_), tinv, coupled)
        half *= 2

    at2 = _each(mm, tinv, at)
    wv = _each(mm, tinv, _each(mm, a_ak, v))
    pn = _each(lambda bd, kd, at2_, wv_, v_: _dot_tn(
        jnp.concatenate([bd, kd], axis=0),
        jnp.concatenate([jnp.concatenate([at2_, wv_], axis=1),
                         jnp.concatenate([jnp.zeros_like(v_), v_], axis=1)], axis=0)), bdec, kdec, at2, wv, v)
    pick = lambda x: jnp.where(in_a, x[0:CHUNK], x[CHUNK:2 * CHUNK])
    p_mat = _each(lambda x: pick(x[:, 0:LANES]), pn)
    n_mat = _each(lambda x: pick(x[:, LANES:2 * LANES]), pn)
    q_mat = _each(lambda rt_, arb, at2_: rt_ + mm(arb, at2_), rt, a_rb, at2)
    y0 = _each(lambda arb, ark, wv_, v_: jnp.dot(
        jnp.concatenate([arb, ark], axis=1).astype(BF16),
        jnp.concatenate([_pair_rhs(wv_, in_a), _pair_rhs(v_, in_a)], axis=0), preferred_element_type=F32),
        a_rb, a_rk, wv, v)
    lanes_of = lambda it: slice(it[1] * LANES, (it[1] + 1) * LANES)
    rk = [rk_ref[:, lanes_of(it)] for it in items]
    sums = _head_sums(_each(lambda el: eye2 * el, e_last) + _each(lambda r_, k_, rk_: r_ * k_ * rk_, r, k, rk),
                      ones2)
    decay, bonus = sums[:len(items)], sums[len(items):]

    y = []
    for i, it in enumerate(items):
        key = it[:2]
        res = jnp.dot(jnp.concatenate([q_mat[i], p_mat[i]], axis=0).astype(BF16), _pair_rhs(st[key], in_a),
                      preferred_element_type=F32)
        y.append(res[0:CHUNK] + y0[i])
        st[key] = decay[i] * st[key] + res[CHUNK:2 * CHUNK] + n_mat[i]

    yc = _each(lambda y_, m_: y_ - m_ * (1.0 / HEAD), y, _head_sums(y, ones2))
    var = _each(lambda x: x * (1.0 / HEAD), _head_sums(_each(lambda x: x * x, yc), ones2))
    return [yc_ * lax.rsqrt(var_ + LNX_EPS) * lw_ref[:, lanes_of(it)] + lb_ref[:, lanes_of(it)] + bonus_ * v_
            for yc_, var_, bonus_, v_, it in zip(yc, var, bonus, v, items)]


def _rwkv_kernel(x_ref, sh0_ref, s0_ref, gain_ref, mu_ref, wrkv_ref, w0_ref, w1_ref, w2_ref, a0_ref, a1_ref, a2_ref,
                 g1_ref, g2_ref, kk_ref, ka_ref, rk_ref, lw_ref, lb_ref,
                 z_ref, g_out, sh_out, s_out, carry_ref, st_ref):
    t = pl.program_id(1)
    n_seq, rows, d = x_ref.shape
    n_sub = rows // CHUNK
    n_pp = d // LANES

    @pl.when(t == 0)
    def _():
        carry_ref[...] = sh0_ref[...]
        st_ref[...] = s0_ref[...]

    x = x_ref[...].reshape(n_seq * rows, d)
    h = _rms(x) * gain_ref[...]
    rolled = pltpu.roll(h, 1, 0)
    first = lax.broadcasted_iota(jnp.int32, (SUBLANES, 1), 0) == 0
    pieces = []
    for q in range(n_seq):
        base = q * rows
        pieces += [jnp.where(first, carry_ref[q], rolled[base:base + SUBLANES]),
                   rolled[base + SUBLANES:base + rows]]
        last = h[base + rows - 1:base + rows, :]
        carry_ref[q] = last
        sh_out[q] = last
    prev = jnp.concatenate(pieces, axis=0)

    ones2 = _head_ones2()
    items = [(q, p, s) for s in range(n_sub) for q in range(n_seq) for p in range(n_pp)]
    vals = {}

    def blk(name, it):
        q, p, s = it
        lo = q * rows + s * CHUNK
        return vals[name][lo:lo + CHUNK, p * LANES:(p + 1) * LANES]

    xx = prev - h
    mix = lambda i: h + xx * mu_ref[i:i + 1, :]
    vals["w"], a_rate = _decay_and_rate(mix, w0_ref, w1_ref, w2_ref, a0_ref, a1_ref, a2_ref)
    vals["r"], vals["k"], vals["v"], vals["a"], vals["b"], g = _receptance_key_value_gate(
        mix, a_rate, wrkv_ref, g1_ref, g2_ref, kk_ref, ka_ref, ones2)
    g_out[...] = g.reshape(n_seq, rows, d)
    decay_factors = _decay_factors([blk("w", it) for it in items])

    st = {(q, p): st_ref[q, :, p * LANES:(p + 1) * LANES] for q in range(n_seq) for p in range(n_pp)}
    z = _wkv_chunks(blk, items, decay_factors, st, rk_ref, lw_ref, lb_ref, ones2)
    for z_, (q, p, s) in zip(z, items):
        z_ref[q, s * CHUNK:(s + 1) * CHUNK, p * LANES:(p + 1) * LANES] = z_
    for (q, p), val in st.items():
        st_ref[q, :, p * LANES:(p + 1) * LANES] = val

    @pl.when(t == pl.num_programs(1) - 1)
    def _():
        for (q, p), val in st.items():
            s_out[q, :, p * LANES:(p + 1) * LANES] = val


def _rwkv_layer(x, shift0, st0, gain, mu, wrkv, w0, w1, w2, a0, a1, a2, g1, g2, k_k, k_a, r_k, lnx_w, lnx_b,
                n_seq, n_sub):
    bsz, t_len, d = x.shape
    rows = n_sub * CHUNK
    row = lambda arr: arr.reshape(1, d)
    const = lambda arr: _resident(arr.shape, lambda b, t: (0,) * arr.ndim)
    seq = pl.BlockSpec((n_seq, rows, d), lambda b, t: (b, t, 0))
    per_seq = lambda n: pl.BlockSpec((n_seq, n, d), lambda b, t: (b, 0, 0))
    args = (x, shift0.reshape(bsz, 1, d), st0, row(gain), mu, wrkv, row(w0), w1, w2, row(a0), a1, a2, g1, g2,
            row(k_k), row(k_a), row(r_k), row(lnx_w), row(lnx_b))
    act = jax.ShapeDtypeStruct((bsz, t_len, d), F32)
    return pl.pallas_call(
        _rwkv_kernel,
        grid=(bsz // n_seq, t_len // rows),
        in_specs=[seq, per_seq(1), per_seq(HEAD)] + [const(a) for a in args[3:]],
        out_specs=[seq, seq, per_seq(1), per_seq(HEAD)],
        out_shape=[act, act, jax.ShapeDtypeStruct((bsz, 1, d), F32), jax.ShapeDtypeStruct((bsz, HEAD, d), F32)],
        scratch_shapes=[pltpu.VMEM((n_seq, 1, d), F32), pltpu.VMEM((n_seq, HEAD, d), F32)],
        compiler_params=pltpu.CompilerParams(dimension_semantics=("arbitrary", "arbitrary"),
                                             vmem_limit_bytes=VMEM_LIMIT_BYTES),
    )(*args)


def _swiglu_residual(x, gain, win_ref, wout_ref, ff_chunk):
    n_ff = wout_ref.shape[0]
    hn = (_rms(x) * gain).astype(BF16)
    acc = x
    for j in range(n_ff // ff_chunk):
        lo = j * ff_chunk
        gate = jnp.dot(hn, win_ref[:, lo:lo + ff_chunk], preferred_element_type=F32)
        up = jnp.dot(hn, win_ref[:, n_ff + lo:n_ff + lo + ff_chunk], preferred_element_type=F32)
        hid = gate * _sigmoid(gate) * up
        acc = acc + jnp.dot(hid.astype(BF16), wout_ref[lo:lo + ff_chunk, :], preferred_element_type=F32)
    return acc


def _rope(x, cs, sn, first):
    out = []
    for p in range(x.shape[1] // LANES):
        xp = x[:, p * LANES:(p + 1) * LANES]
        partner = jnp.where(first, pltpu.roll(xp, LANES - ROT_DIM // 2, 1), pltpu.roll(xp, ROT_DIM // 2, 1))
        out.append(xp * cs + partner * sn)
    return jnp.concatenate(out, axis=1)


def _ffn_qkv_kernel(z_ref, g_ref, x_ref, wo_ref, gain_ref, win_ref, wout_ref, gq_ref, gkv_ref, wq_ref, bq_ref,
                    wkv_ref, bkv_ref, inv_ref, x_out, q_out, k_out, v_out, *tails, ff_chunk, seq_len, pos_base):
    x1 = x_ref[...] + _dot(z_ref[...] * g_ref[...], wo_ref[...])
    x2 = _swiglu_residual(x1, gain_ref[...], win_ref, wout_ref, ff_chunk)
    x_out[...] = x2

    tm = x2.shape[0]
    kvw = k_out.shape[1]
    y = _rms(x2)
    q = _dot(y * gq_ref[...], wq_ref[...]) + bq_ref[...]
    kv = _dot(y * gkv_ref[...], wkv_ref[...]) + bkv_ref[...]
    rows = lax.broadcasted_iota(jnp.int32, (tm, 1), 0) + pl.program_id(0) * tm
    pos = (lax.rem(rows, seq_len) + pos_base).astype(F32)
    ang = pos * inv_ref[...]
    lane = lax.broadcasted_iota(jnp.int32, (1, LANES), 1) % HEAD
    first = lane < ROT_DIM // 2
    cs = jnp.cos(ang)
    sn = jnp.where(first, -1.0, 1.0) * jnp.sin(ang)
    q_out[...] = _rope(q, cs, sn, first)
    k = _rope(kv[:, :kvw], cs, sn, first)
    k_out[...] = k
    v_out[...] = kv[:, kvw:]
    if tails:
        tails[0][0] = k[tm - WINDOW:, :]
        tails[1][0] = kv[tm - WINDOW:, kvw:]


def _ffn_qkv(z, g, x, w_o, gain, w_in, w_out, gain_q, gain_kv, w_q, b_q, w_kv, b_kv, seq_len, pos_base, tm,
             ff_chunk, layer):
    n, d = x.shape
    kvw = w_kv.shape[1] // 2
    half = ROT_DIM // 2
    lane = jnp.arange(LANES) % HEAD
    inv = jnp.power(jnp.float32(ROPE_THETA), -(lane % half).astype(F32) * (2.0 / ROT_DIM))
    inv = jnp.where(lane < ROT_DIM, inv, 0.0).reshape(1, LANES)
    tile = lambda w: pl.BlockSpec((tm, w), lambda i: (i, 0))
    const = lambda arr: _weight_spec(arr, layer)
    consts = (w_o, gain.reshape(1, d), w_in, w_out, gain_q.reshape(1, d), gain_kv.reshape(1, d), w_q,
              b_q.reshape(1, -1), w_kv, b_kv.reshape(1, -1), inv)
    qw = w_q.shape[1]
    out_specs = [tile(d), tile(qw), tile(kvw), tile(kvw)]
    out_shape = [jax.ShapeDtypeStruct((n, d), F32), jax.ShapeDtypeStruct((n, qw), F32),
                 jax.ShapeDtypeStruct((n, kvw), F32), jax.ShapeDtypeStruct((n, kvw), F32)]
    if seq_len % tm == 0 and tm >= WINDOW:
        tiles_per_seq = seq_len // tm
        tail = pl.BlockSpec((1, WINDOW, kvw), lambda i: (i // tiles_per_seq, 0, 0))
        out_specs += [tail, tail]
        out_shape += [jax.ShapeDtypeStruct((n // seq_len, WINDOW, kvw), F32)] * 2
    return pl.pallas_call(
        functools.partial(_ffn_qkv_kernel, ff_chunk=ff_chunk, seq_len=seq_len, pos_base=pos_base),
        grid=(n // tm,),
        in_specs=[tile(d)] * 3 + [const(a) for a in consts],
        out_specs=out_specs,
        out_shape=out_shape,
        compiler_params=pltpu.CompilerParams(dimension_semantics=("arbitrary",),
                                             vmem_limit_bytes=VMEM_LIMIT_BYTES),
    )(z, g, x, *consts)


def _dup_heads(kp, in_a):
    rolled = pltpu.roll(kp, HEAD, 1)
    return jnp.where(in_a, kp, rolled), jnp.where(in_a, rolled, kp)


def _attend(q_chunks, keys, vals, valids, sink_ref):
    n_kv = len(keys[0])
    groups_per_kv = q_chunks[0].shape[1] // LANES // n_kv
    in_a = lax.broadcasted_iota(jnp.int32, (CHUNK, LANES), 1) < HEAD
    n_keys = WINDOW + CHUNK
    width = keys[0][0].shape[0]
    col = lax.broadcasted_iota(jnp.int32, (1, width), 1)
    real_key = col < n_keys
    real_row = lax.broadcasted_iota(jnp.int32, (width, 1), 0) < n_keys
    ones = jnp.ones((width, LANES), BF16)
    inst = [(c, hk) for c in range(len(q_chunks)) for hk in range(n_kv)]

    fill = []
    for hk in range(n_kv):
        blocks = [jnp.full((CHUNK, width), sink_ref[hk * 2 * groups_per_kv + j], F32)
                  for j in range(2 * groups_per_kv)]
        fill.append(jnp.where(col == n_keys, jnp.concatenate(blocks, axis=0), NEG_INF))

    def scores(c, hk):
        blocks = []
        for m in range(groups_per_kv):
            lo = (hk * groups_per_kv + m) * LANES
            qg = q_chunks[c][:, lo:lo + LANES] * ATTN_SCALE
            zero = jnp.zeros_like(qg)
            blocks += [jnp.where(in_a, qg, zero), jnp.where(in_a, zero, qg)]
        s = _dot_nt(jnp.concatenate(blocks, axis=0), keys[c][hk])
        keep = real_key if valids[c] is None else jnp.logical_and(real_key, valids[c])
        return jnp.where(keep, s, fill[hk])

    s = [scores(c, hk) for c, hk in inst]
    p = [jnp.exp(x - jnp.max(x, axis=-1, keepdims=True)).astype(BF16) for x in s]
    v_ext = [jnp.concatenate([jnp.where(real_row, vals[c][hk], 0.0).astype(BF16), ones], axis=1)
             for c, hk in inst]
    pv = [jnp.dot(x, v, preferred_element_type=F32) for x, v in zip(p, v_ext)]
    o = [x[:, :LANES] / x[:, LANES:] for x in pv]
    rows = []
    for c in range(len(q_chunks)):
        groups = []
        for hk in range(n_kv):
            oc = o[c * n_kv + hk]
            for m in range(groups_per_kv):
                groups.append(jnp.where(in_a, oc[2 * m * CHUNK:(2 * m + 1) * CHUNK],
                                        oc[(2 * m + 1) * CHUNK:(2 * m + 2) * CHUNK]))
        rows.append(jnp.concatenate(groups, axis=1))
    return jnp.concatenate(rows, axis=0)


def _attn_ffn_kernel(sink_ref, q_ref, kc_ref, vc_ref, kp_ref, vp_ref, x_ref, wo_ref, bo_ref, gain_ref,
                     win_ref, wout_ref, gfin_ref, o_ref, *, streaming, ff_chunk):
    tm = x_ref.shape[-2]
    n_chunks = tm // CHUNK
    kvw = kc_ref.shape[-1]
    n_keys = WINDOW + CHUNK

    def dup(arr):
        in_a = lax.broadcasted_iota(jnp.int32, (arr.shape[0], LANES), 1) < HEAD
        res = []
        for j in range(kvw // LANES):
            res += list(_dup_heads(arr[:, j * LANES:(j + 1) * LANES], in_a))
        return res

    filler = jnp.zeros((CHUNK, kvw), F32)
    if streaming:
        q = q_ref[0]
        k_all = dup(jnp.concatenate([kp_ref[0], kc_ref[0], filler], axis=0))
        v_all = dup(jnp.concatenate([vp_ref[0], vc_ref[0], filler], axis=0))
        first_tile = pl.program_id(1) == 0
        key_idx = lax.broadcasted_iota(jnp.int32, (1, n_keys + CHUNK), 1)
    else:
        q = q_ref[...]

    q_chunks, keys, vals, valids = [], [], [], []
    for c in range(n_chunks):
        q_chunks.append(q[c * CHUNK:(c + 1) * CHUNK])
        valid = None
        if streaming:
            keys.append([ka[c * CHUNK:c * CHUNK + n_keys + CHUNK] for ka in k_all])
            vals.append([va[c * CHUNK:c * CHUNK + n_keys + CHUNK] for va in v_all])
            if c * CHUNK < WINDOW:
                valid = jnp.logical_or(key_idx + c * CHUNK >= WINDOW, jnp.logical_not(first_tile))
        else:
            keys.append(dup(jnp.concatenate([kp_ref[c], kc_ref[c * CHUNK:(c + 1) * CHUNK], filler], axis=0)))
            vals.append(dup(jnp.concatenate([vp_ref[c], vc_ref[c * CHUNK:(c + 1) * CHUNK], filler], axis=0)))
        valids.append(valid)
    o = _attend(q_chunks, keys, vals, valids, sink_ref)

    x = x_ref[0] if streaming else x_ref[...]
    x1 = x + _dot(o, wo_ref[...]) + bo_ref[...]
    x2 = _swiglu_residual(x1, gain_ref[...], win_ref, wout_ref, ff_chunk)
    y = _rms(x2) * gfin_ref[...]
    if streaming:
        o_ref[0] = y
    else:
        o_ref[...] = y


def _attn_ffn(q, k_new, v_new, k_prev, v_prev, x, sinks, w_o, b_o, gain, w_in, w_out, gain_fin, tm, ff_chunk,
              layer):
    d = x.shape[-1]
    kvw = k_new.shape[-1]
    streaming = k_prev is None
    smem = pl.BlockSpec(memory_space=pltpu.SMEM)
    weights = (w_o, b_o.reshape(1, d), gain.reshape(1, d), w_in, w_out, gain_fin.reshape(1, d))
    if streaming:
        bsz, t_len, _ = x.shape
        assert tm % WINDOW == 0
        per_win = tm // WINDOW
        grid = (bsz, t_len // tm)
        tile = lambda w: pl.BlockSpec((1, tm, w), lambda b, t: (b, t, 0))
        prev = pl.BlockSpec((1, WINDOW, kvw), lambda b, t: (b, jnp.maximum(t * per_win - 1, 0), 0))
        in_specs = ([smem, tile(d), tile(kvw), tile(kvw), prev, prev, tile(d)]
                    + [_weight_spec(a, layer) for a in weights])
        args = (sinks, q, k_new, v_new, k_new, v_new, x) + weights
        out_spec = tile(d)
        sem = ("arbitrary", "arbitrary")
    else:
        n = x.shape[0]
        grid = (1,)
        full = lambda arr: pl.BlockSpec(arr.shape, lambda i: (0,) * arr.ndim)
        args = (sinks, q, k_new, v_new, k_prev, v_prev, x) + weights
        in_specs = [smem] + [full(a) for a in args[1:7]] + [_weight_spec(a, layer) for a in weights]
        out_spec = pl.BlockSpec((n, d), lambda i: (0, 0))
        sem = ("arbitrary",)
    return pl.pallas_call(
        functools.partial(_attn_ffn_kernel, streaming=streaming, ff_chunk=ff_chunk),
        grid=grid,
        in_specs=in_specs,
        out_specs=out_spec,
        out_shape=jax.ShapeDtypeStruct(x.shape, F32),
        compiler_params=pltpu.CompilerParams(dimension_semantics=sem, vmem_limit_bytes=VMEM_LIMIT_BYTES),
    )(*args)


def _tile(n, cap):
    t = min(n, cap)
    while n % t:
        t -= 1
    return t


def kernel(x_prompt, x_sample, state_wkv, state_shift, cache_k, cache_v, norm_mix, norm_ffn, rw_mu, rw_w_rkv, rw_w0, rw_w1, rw_w2, rw_a0, rw_a1, rw_a2, rw_g1, rw_g2, rw_k_k, rw_k_a, rw_r_k, rw_lnx_w, rw_lnx_b, rw_w_o, kv_norm, w_kv, b_kv, w_q, b_q, attn_sinks, w_o, b_o, ffn_w_in, ffn_w_out, norm_final):
    depth, d = norm_mix.shape
    assert depth == 2 and rw_mu.shape[0] == 1 and w_q.shape[0] == 1, "one RWKV layer followed by one attention layer"
    n_heads = d // HEAD
    n_ff = ffn_w_out.shape[1]
    ff_chunk = FF_SLICE if n_ff % FF_SLICE == 0 else n_ff
    bf = lambda w: w.astype(BF16)
    wrkv, w1, w2, a1, a2, g1, g2 = map(bf, (rw_w_rkv[0], rw_w1[0], rw_w2[0], rw_a1[0], rw_a2[0], rw_g1[0], rw_g2[0]))
    rw_wo, wq, wkv, wo = bf(rw_w_o[0]), bf(w_q[0]), bf(w_kv), bf(w_o[0])
    win, wout = bf(ffn_w_in), bf(ffn_w_out)
    sinks = attn_sinks[0].reshape(-1)

    def run(x, pos_base, shift0, st0, win_k, win_v):
        bsz, t_len, _ = x.shape
        n_rows = bsz * t_len
        n_sub = _tile(t_len // CHUNK, RWKV_CHUNKS_PER_STEP)
        z, g, shift, st = _rwkv_layer(
            x, shift0, st0, norm_mix[0], rw_mu[0], wrkv, rw_w0[0], w1, w2, rw_a0[0], a1, a2, g1, g2,
            rw_k_k[0], rw_k_a[0], rw_r_k[0].reshape(-1), rw_lnx_w[0], rw_lnx_b[0],
            n_seq=_tile(bsz, RWKV_CHUNKS_PER_STEP // n_sub), n_sub=n_sub)
        flat = lambda t: t.reshape(n_rows, t.shape[-1])
        x1, q, k_new, v_new, *tails = _ffn_qkv(
            flat(z), flat(g), flat(x), rw_wo, norm_ffn[0], win, wout, norm_mix[1], kv_norm, wq, b_q[0],
            wkv, b_kv, t_len, pos_base, tm=_tile(n_rows, ROW_TILE), ff_chunk=ff_chunk, layer=0)
        kvw = k_new.shape[-1]
        k_new = k_new.reshape(bsz, t_len, kvw)
        v_new = v_new.reshape(bsz, t_len, kvw)
        if win_k is None:
            y = _attn_ffn(q.reshape(bsz, t_len, d), k_new, v_new, None, None, x1.reshape(bsz, t_len, d), sinks,
                          wo, b_o[0], norm_ffn[1], win, wout, norm_final, tm=_tile(t_len, ROW_TILE),
                          ff_chunk=ff_chunk, layer=1)
            k_state, v_state = tails if tails else (k_new[:, -WINDOW:], v_new[:, -WINDOW:])
        else:
            assert t_len == CHUNK
            kp = win_k.reshape(bsz, WINDOW, kvw)
            vp = win_v.reshape(bsz, WINDOW, kvw)
            y = _attn_ffn(q, k_new.reshape(n_rows, kvw), v_new.reshape(n_rows, kvw), kp, vp, x1, sinks,
                          wo, b_o[0], norm_ffn[1], win, wout, norm_final, tm=n_rows, ff_chunk=ff_chunk,
                          layer=1)
            k_state = jnp.concatenate([kp, k_new], axis=1)[:, -WINDOW:]
            v_state = jnp.concatenate([vp, v_new], axis=1)[:, -WINDOW:]
        n_kv = kvw // HEAD
        st = st.reshape(bsz, HEAD, n_heads, HEAD).transpose(0, 2, 3, 1)[None]
        return (y.reshape(bsz, t_len, d), st, shift.reshape(1, bsz, d),
                k_state.reshape(bsz, WINDOW, n_kv, HEAD), v_state.reshape(bsz, WINDOW, n_kv, HEAD))

    bp = x_prompt.shape[0]
    zeros_state = jnp.zeros((bp, HEAD, d), F32)
    y_p, wkv_p, shift_p, k_p, v_p = run(x_prompt, 0, jnp.zeros((bp, d), F32), zeros_state, None, None)
    st_s = state_wkv[0].transpose(0, 3, 1, 2).reshape(x_sample.shape[0], HEAD, d)
    y_s, wkv_s, shift_s, k_s, v_s = run(x_sample, PAST_LEN, state_shift[0], st_s, cache_k, cache_v)
    return (y_p, y_s, wkv_p, shift_p, k_p, v_p, wkv_s, shift_s, k_s, v_s)
```

```python
import functools

import jax
import jax.numpy as jnp
from jax import lax
from jax.experimental import pallas as pl
from jax.experimental.pallas import tpu as pltpu

F32 = jnp.float32
BF16 = jnp.bfloat16

HEAD = 64
LANES = 128
SUBLANES = 8
CHUNK = 64
WINDOW = 128
ROT_DIM = HEAD // 4
ROPE_THETA = 500000.0
RMS_EPS = 1e-5
LNX_EPS = 64e-5
NEG_INF = -1e30
ATTN_SCALE = 1.0 / (HEAD ** 0.5)
VMEM_LIMIT_BYTES = 56 * 1024 * 1024
PAST_LEN = 1024
RWKV_CHUNKS_PER_STEP = 8
ROW_TILE = 512
FF_SLICE = 256


def _resident(shape, index_map):
    return pl.BlockSpec(shape, index_map, pipeline_mode=pl.Buffered(1))


def _weight_spec(arr, layer):
    if arr.ndim == 2:
        return _resident(arr.shape, lambda *_: (0, 0))
    return _resident((None,) + arr.shape[1:], lambda *_: (layer, 0, 0))


def _dot(a, b):
    return jnp.dot(a.astype(BF16), b.astype(BF16), preferred_element_type=F32)


def _dot_nt(a, b):
    return lax.dot_general(a.astype(BF16), b.astype(BF16), (((1,), (1,)), ((), ())),
                           preferred_element_type=F32)


def _dot_tn(a, b):
    return lax.dot_general(a.astype(BF16), b.astype(BF16), (((0,), (0,)), ((), ())),
                           preferred_element_type=F32)


def _split2(x):
    hi = x.astype(BF16)
    return hi, (x - hi.astype(F32)).astype(BF16)


def _rms(x):
    return x * lax.rsqrt(jnp.mean(x * x, axis=-1, keepdims=True) + RMS_EPS)


def _sigmoid(x):
    return 1.0 / (1.0 + jnp.exp(-x))


def _head_ones2():
    r = lax.broadcasted_iota(jnp.int32, (2 * LANES, LANES), 0)
    c = lax.broadcasted_iota(jnp.int32, (2 * LANES, LANES), 1)
    return jnp.where(((r % LANES) // HEAD) == (c // HEAD), 1.0, 0.0).astype(BF16)


def _head_sums(xs, ones2):
    lhs = jnp.concatenate([jnp.concatenate(list(_split2(x)), axis=1) for x in xs], axis=0)
    out = jnp.dot(lhs, ones2, preferred_element_type=F32)
    res, lo = [], 0
    for x in xs:
        res.append(out[lo:lo + x.shape[0]])
        lo += x.shape[0]
    return res


def _pair_rhs(q, in_a):
    zero = jnp.zeros_like(q)
    return jnp.concatenate([jnp.where(in_a, q, zero), jnp.where(in_a, zero, q)], axis=0).astype(BF16)


def _pairmm(p, q, in_a):
    return jnp.dot(p.astype(BF16), _pair_rhs(q, in_a), preferred_element_type=F32)


def _each(fn, *cols):
    return [fn(*xs) for xs in zip(*cols)]


def _decay_and_rate(mix, w0_ref, w1_ref, w2_ref, a0_ref, a1_ref, a2_ref):
    wl = w0_ref[...] + _dot(jnp.tanh(_dot(mix(1), w1_ref[...])), w2_ref[...])
    sp = jnp.maximum(-wl, 0.0) + jnp.log(1.0 + jnp.exp(-jnp.abs(wl)))
    logdecay = -jnp.exp(-sp - 0.5)
    a = _sigmoid(a0_ref[...] + _dot(_dot(mix(4), a1_ref[...]), a2_ref[...]))
    return logdecay, a


def _decay_factors(w_blocks):
    shape = (CHUNK, LANES)
    incl = lax.broadcasted_iota(jnp.int32, shape, 1) % HEAD <= lax.broadcasted_iota(jnp.int32, shape, 0)
    tri2 = jnp.where(incl, 1.0, 0.0).astype(BF16)
    g = _each(lambda x: jnp.dot(tri2, jnp.concatenate(list(_split2(x)), axis=0), preferred_element_type=F32),
              w_blocks)
    return _each(jnp.exp, g), _each(lambda x: jnp.exp(-x), g), _each(lambda x: jnp.exp(-x), w_blocks)


def _receptance_key_value_gate(mix, a, wrkv_ref, g1_ref, g2_ref, kk_ref, ka_ref, ones2):
    k = _dot(mix(2), wrkv_ref[1])
    r = _dot(mix(0), wrkv_ref[0])
    v = _dot(mix(3), wrkv_ref[2])
    g = _dot(_sigmoid(_dot(mix(5), g1_ref[...])), g2_ref[...])
    kk = k * kk_ref[...]
    groups = [kk[:, p * LANES:(p + 1) * LANES] for p in range(k.shape[1] // LANES)]
    ss = _head_sums([x * x for x in groups], ones2)
    kkn = jnp.concatenate([x * lax.rsqrt(jnp.maximum(s, 1e-24)) for x, s in zip(groups, ss)], axis=1)
    return r, k * (1.0 + (a - 1.0) * ka_ref[...]), v, -kkn, kkn * a, g


def _wkv_chunks(blk, items, decay_factors, st, rk_ref, lw_ref, lb_ref, ones2):
    shape = (CHUNK, LANES)
    row = lax.broadcasted_iota(jnp.int32, shape, 0)
    lane = lax.broadcasted_iota(jnp.int32, shape, 1)
    in_a = lane < HEAD
    li = lane % HEAD
    strict = li < row
    incl = li <= row
    eye2 = jnp.where(li == row, 1.0, 0.0).astype(F32)
    mm = lambda p_, q_: _pairmm(p_, q_, in_a)

    r, k, v, a, b = ([blk(name, it) for it in items] for name in "rkvab")
    e, ei, ew = decay_factors
    at = _each(lambda a_, e_, ew_: a_ * e_ * ew_, a, e, ew)
    rt = _each(lambda r_, e_: r_ * e_, r, e)
    bt = _each(lambda b_, x: b_ * x, b, ei)
    kt = _each(lambda k_, x: k_ * x, k, ei)
    e_last = _each(lambda e_: e_[CHUNK - 1:CHUNK, :], e)
    bdec = _each(lambda x, el: x * el, bt, e_last)
    kdec = _each(lambda x, el: x * el, kt, e_last)

    def pair_products(at_, rt_, bt_, kt_):
        zero = jnp.zeros_like(bt_)
        rhs4 = jnp.concatenate([jnp.where(in_a, bt_, zero), jnp.where(in_a, zero, bt_),
                                jnp.where(in_a, kt_, zero), jnp.where(in_a, zero, kt_)], axis=0)
        return _dot_nt(jnp.concatenate([at_, rt_], axis=0), rhs4)

    aa = _each(pair_products, at, rt, bt, kt)
    a_ab = _each(lambda x: jnp.where(strict, x[0:CHUNK, 0:LANES], 0.0), aa)
    a_ak = _each(lambda x: jnp.where(strict, x[0:CHUNK, LANES:2 * LANES], 0.0), aa)
    a_rb = _each(lambda x: jnp.where(incl, x[CHUNK:2 * CHUNK, 0:LANES], 0.0), aa)
    a_rk = _each(lambda x: jnp.where(incl, x[CHUNK:2 * CHUNK, LANES:2 * LANES], 0.0), aa)

    tinv = _each(lambda x: eye2 + jnp.where(row // 2 == li // 2, x, 0.0), a_ab)
    half = 2
    while half < CHUNK:
        lower_left = (row // (2 * half) == li // (2 * half)) & (row // half != li // half)
        coupled = _each(mm, _each(lambda x: jnp.where(lower_left, x, 0.0), a_ab), tinv)
        tinv = _each(lambda t_, y_: t_ + mm(t_, y_), tinv, coupled)
        half *= 2

    at2 = _each(mm, tinv, at)
    wv = _each(mm, tinv, _each(mm, a_ak, v))
    pn = _each(lambda bd, kd, at2_, wv_, v_: _dot_tn(
        jnp.concatenate([bd, kd], axis=0),
        jnp.concatenate([jnp.concatenate([at2_, wv_], axis=1),
                         jnp.concatenate([jnp.zeros_like(v_), v_], axis=1)], axis=0)), bdec, kdec, at2, wv, v)
    pick = lambda x: jnp.where(in_a, x[0:CHUNK], x[CHUNK:2 * CHUNK])
    p_mat = _each(lambda x: pick(x[:, 0:LANES]), pn)
    n_mat = _each(lambda x: pick(x[:, LANES:2 * LANES]), pn)
    q_mat = _each(lambda rt_, arb, at2_: rt_ + mm(arb, at2_), rt, a_rb, at2)
    y0 = _each(lambda arb, ark, wv_, v_: jnp.dot(
        jnp.concatenate([arb, ark], axis=1).astype(BF16),
        jnp.concatenate([_pair_rhs(wv_, in_a), _pair_rhs(v_, in_a)], axis=0), preferred_element_type=F32),
        a_rb, a_rk, wv, v)
    lanes_of = lambda it: slice(it[1] * LANES, (it[1] + 1) * LANES)
    rk = [rk_ref[:, lanes_of(it)] for it in items]
    sums = _head_sums(_each(lambda el: eye2 * el, e_last) + _each(lambda r_, k_, rk_: r_ * k_ * rk_, r, k, rk),
                      ones2)
    decay, bonus = sums[:len(items)], sums[len(items):]

    y = []
    for i, it in enumerate(items):
        key = it[:2]
        res = jnp.dot(jnp.concatenate([q_mat[i], p_mat[i]], axis=0).astype(BF16), _pair_rhs(st[key], in_a),
                      preferred_element_type=F32)
        y.append(res[0:CHUNK] + y0[i])
        st[key] = decay[i] * st[key] + res[CHUNK:2 * CHUNK] + n_mat[i]

    yc = _each(lambda y_, m_: y_ - m_ * (1.0 / HEAD), y, _head_sums(y, ones2))
    var = _each(lambda x: x * (1.0 / HEAD), _head_sums(_each(lambda x: x * x, yc), ones2))
    return [yc_ * lax.rsqrt(var_ + LNX_EPS) * lw_ref[:, lanes_of(it)] + lb_ref[:, lanes_of(it)] + bonus_ * v_
            for yc_, var_, bonus_, v_, it in zip(yc, var, bonus, v, items)]


def _rwkv_kernel(x_ref, sh0_ref, s0_ref, gain_ref, mu_ref, wrkv_ref, w0_ref, w1_ref, w2_ref, a0_ref, a1_ref, a2_ref,
                 g1_ref, g2_ref, kk_ref, ka_ref, rk_ref, lw_ref, lb_ref,
                 z_ref, g_out, sh_out, s_out, carry_ref, st_ref):
    t = pl.program_id(1)
    n_seq, rows, d = x_ref.shape
    n_sub = rows // CHUNK
    n_pp = d // LANES

    @pl.when(t == 0)
    def _():
        carry_ref[...] = sh0_ref[...]
        st_ref[...] = s0_ref[...]

    x = x_ref[...].reshape(n_seq * rows, d)
    h = _rms(x) * gain_ref[...]
    rolled = pltpu.roll(h, 1, 0)
    first = lax.broadcasted_iota(jnp.int32, (SUBLANES, 1), 0) == 0
    pieces = []
    for q in range(n_seq):
        base = q * rows
        pieces += [jnp.where(first, carry_ref[q], rolled[base:base + SUBLANES]),
                   rolled[base + SUBLANES:base + rows]]
        last = h[base + rows - 1:base + rows, :]
        carry_ref[q] = last
        sh_out[q] = last
    prev = jnp.concatenate(pieces, axis=0)

    ones2 = _head_ones2()
    items = [(q, p, s) for s in range(n_sub) for q in range(n_seq) for p in range(n_pp)]
    vals = {}

    def blk(name, it):
        q, p, s = it
        lo = q * rows + s * CHUNK
        return vals[name][lo:lo + CHUNK, p * LANES:(p + 1) * LANES]

    xx = prev - h
    mix = lambda i: h + xx * mu_ref[i:i + 1, :]
    vals["w"], a_rate = _decay_and_rate(mix, w0_ref, w1_ref, w2_ref, a0_ref, a1_ref, a2_ref)
    vals["r"], vals["k"], vals["v"], vals["a"], vals["b"], g = _receptance_key_value_gate(
        mix, a_rate, wrkv_ref, g1_ref, g2_ref, kk_ref, ka_ref, ones2)
    g_out[...] = g.reshape(n_seq, rows, d)
    decay_factors = _decay_factors([blk("w", it) for it in items])

    st = {(q, p): st_ref[q, :, p * LANES:(p + 1) * LANES] for q in range(n_seq) for p in range(n_pp)}
    z = _wkv_chunks(blk, items, decay_factors, st, rk_ref, lw_ref, lb_ref, ones2)
    for z_, (q, p, s) in zip(z, items):
        z_ref[q, s * CHUNK:(s + 1) * CHUNK, p * LANES:(p + 1) * LANES] = z_
    for (q, p), val in st.items():
        st_ref[q, :, p * LANES:(p + 1) * LANES] = val

    @pl.when(t == pl.num_programs(1) - 1)
    def _():
        r3 = lax.broadcasted_iota(jnp.int32, (3 * HEAD, LANES), 0)
        c3 = lax.broadcasted_iota(jnp.int32, (3 * HEAD, LANES), 1)
        eye3 = jnp.where(r3 % HEAD == c3, 1.0, 0.0).astype(BF16)
        for (q, p), val in st.items():
            p1 = val.astype(BF16)
            r1 = val - p1.astype(F32)
            p2 = r1.astype(BF16)
            p3 = (r1 - p2.astype(F32)).astype(BF16)
            tr = lax.dot_general(jnp.concatenate([p1, p2, p3], axis=0), eye3, (((0,), (0,)), ((), ())),
                                 preferred_element_type=F32)
            s_out[q, 2 * p] = tr[0:HEAD, 0:HEAD]
            s_out[q, 2 * p + 1] = tr[HEAD:2 * HEAD, 0:HEAD]


def _rwkv_layer(x, shift0, st0, gain, mu, wrkv, w0, w1, w2, a0, a1, a2, g1, g2, k_k, k_a, r_k, lnx_w, lnx_b,
                n_seq, n_sub):
    bsz, t_len, d = x.shape
    rows = n_sub * CHUNK
    row = lambda arr: arr.reshape(1, d)
    const = lambda arr: _resident(arr.shape, lambda b, t: (0,) * arr.ndim)
    seq = pl.BlockSpec((n_seq, rows, d), lambda b, t: (b, t, 0))
    per_seq = lambda n: pl.BlockSpec((n_seq, n, d), lambda b, t: (b, 0, 0))
    args = (x, shift0.reshape(bsz, 1, d), st0, row(gain), mu, wrkv, row(w0), w1, w2, row(a0), a1, a2, g1, g2,
            row(k_k), row(k_a), row(r_k), row(lnx_w), row(lnx_b))
    act = jax.ShapeDtypeStruct((bsz, t_len, d), F32)
    return pl.pallas_call(
        _rwkv_kernel,
        grid=(bsz // n_seq, t_len // rows),
        in_specs=[seq, per_seq(1), per_seq(HEAD)] + [const(a) for a in args[3:]],
        out_specs=[seq, seq, per_seq(1),
                   pl.BlockSpec((n_seq, d // HEAD, HEAD, HEAD), lambda b, t: (b, 0, 0, 0))],
        out_shape=[act, act, jax.ShapeDtypeStruct((bsz, 1, d), F32),
                   jax.ShapeDtypeStruct((bsz, d // HEAD, HEAD, HEAD), F32)],
        scratch_shapes=[pltpu.VMEM((n_seq, 1, d), F32), pltpu.VMEM((n_seq, HEAD, d), F32)],
        compiler_params=pltpu.CompilerParams(dimension_semantics=("arbitrary", "arbitrary"),
                                             vmem_limit_bytes=VMEM_LIMIT_BYTES),
    )(*args)


def _swiglu_residual(x, gain, win_ref, wout_ref, ff_chunk):
    n_ff = wout_ref.shape[0]
    hn = (_rms(x) * gain).astype(BF16)
    acc = x
    for j in range(n_ff // ff_chunk):
        lo = j * ff_chunk
        gate = jnp.dot(hn, win_ref[:, lo:lo + ff_chunk], preferred_element_type=F32)
        up = jnp.dot(hn, win_ref[:, n_ff + lo:n_ff + lo + ff_chunk], preferred_element_type=F32)
        hid = gate * _sigmoid(gate) * up
        acc = acc + jnp.dot(hid.astype(BF16), wout_ref[lo:lo + ff_chunk, :], preferred_element_type=F32)
    return acc


def _rope(x, cs, sn, first):
    out = []
    for p in range(x.shape[1] // LANES):
        xp = x[:, p * LANES:(p + 1) * LANES]
        partner = jnp.where(first, pltpu.roll(xp, LANES - ROT_DIM // 2, 1), pltpu.roll(xp, ROT_DIM // 2, 1))
        out.append(xp * cs + partner * sn)
    return jnp.concatenate(out, axis=1)


def _ffn_qkv_kernel(z_ref, g_ref, x_ref, wo_ref, gain_ref, win_ref, wout_ref, gq_ref, gkv_ref, wq_ref, bq_ref,
                    wkv_ref, bkv_ref, inv_ref, x_out, q_out, k_out, v_out, *tails, ff_chunk, seq_len, pos_base):
    x1 = x_ref[...] + _dot(z_ref[...] * g_ref[...], wo_ref[...])
    x2 = _swiglu_residual(x1, gain_ref[...], win_ref, wout_ref, ff_chunk)
    x_out[...] = x2

    tm = x2.shape[0]
    kvw = k_out.shape[1]
    y = _rms(x2)
    q = _dot(y * gq_ref[...], wq_ref[...]) + bq_ref[...]
    kv = _dot(y * gkv_ref[...], wkv_ref[...]) + bkv_ref[...]
    rows = lax.broadcasted_iota(jnp.int32, (tm, 1), 0) + pl.program_id(0) * tm
    pos = (lax.rem(rows, seq_len) + pos_base).astype(F32)
    ang = pos * inv_ref[...]
    lane = lax.broadcasted_iota(jnp.int32, (1, LANES), 1) % HEAD
    first = lane < ROT_DIM // 2
    cs = jnp.cos(ang)
    sn = jnp.where(first, -1.0, 1.0) * jnp.sin(ang)
    q_out[...] = _rope(q, cs, sn, first)
    k = _rope(kv[:, :kvw], cs, sn, first)
    k_out[...] = k
    v_out[...] = kv[:, kvw:]
    if tails:
        tails[0][0] = k[tm - WINDOW:, :]
        tails[1][0] = kv[tm - WINDOW:, kvw:]


def _ffn_qkv(z, g, x, w_o, gain, w_in, w_out, gain_q, gain_kv, w_q, b_q, w_kv, b_kv, seq_len, pos_base, tm,
             ff_chunk, layer):
    n, d = x.shape
    kvw = w_kv.shape[1] // 2
    half = ROT_DIM // 2
    lane = jnp.arange(LANES) % HEAD
    inv = jnp.power(jnp.float32(ROPE_THETA), -(lane % half).astype(F32) * (2.0 / ROT_DIM))
    inv = jnp.where(lane < ROT_DIM, inv, 0.0).reshape(1, LANES)
    tile = lambda w: pl.BlockSpec((tm, w), lambda i: (i, 0))
    const = lambda arr: _weight_spec(arr, layer)
    consts = (w_o, gain.reshape(1, d), w_in, w_out, gain_q.reshape(1, d), gain_kv.reshape(1, d), w_q,
              b_q.reshape(1, -1), w_kv, b_kv.reshape(1, -1), inv)
    qw = w_q.shape[1]
    out_specs = [tile(d), tile(qw), tile(kvw), tile(kvw)]
    out_shape = [jax.ShapeDtypeStruct((n, d), F32), jax.ShapeDtypeStruct((n, qw), F32),
                 jax.ShapeDtypeStruct((n, kvw), F32), jax.ShapeDtypeStruct((n, kvw), F32)]
    if seq_len % tm == 0 and tm >= WINDOW:
        tiles_per_seq = seq_len // tm
        tail = pl.BlockSpec((1, WINDOW, kvw), lambda i: (i // tiles_per_seq, 0, 0))
        out_specs += [tail, tail]
        out_shape += [jax.ShapeDtypeStruct((n // seq_len, WINDOW, kvw), F32)] * 2
    return pl.pallas_call(
        functools.partial(_ffn_qkv_kernel, ff_chunk=ff_chunk, seq_len=seq_len, pos_base=pos_base),
        grid=(n // tm,),
        in_specs=[tile(d)] * 3 + [const(a) for a in consts],
        out_specs=out_specs,
        out_shape=out_shape,
        compiler_params=pltpu.CompilerParams(dimension_semantics=("arbitrary",),
                                             vmem_limit_bytes=VMEM_LIMIT_BYTES),
    )(z, g, x, *consts)


def _dup_heads(kp, in_a):
    rolled = pltpu.roll(kp, HEAD, 1)
    return jnp.where(in_a, kp, rolled), jnp.where(in_a, rolled, kp)


def _attend(q_chunks, keys, vals, valids, sink_ref):
    n_kv = len(keys[0])
    groups_per_kv = q_chunks[0].shape[1] // LANES // n_kv
    in_a = lax.broadcasted_iota(jnp.int32, (CHUNK, LANES), 1) < HEAD
    n_keys = WINDOW + CHUNK
    width = keys[0][0].shape[0]
    col = lax.broadcasted_iota(jnp.int32, (1, width), 1)
    real_key = col < n_keys
    real_row = lax.broadcasted_iota(jnp.int32, (width, 1), 0) < n_keys
    ones = jnp.ones((width, LANES), BF16)
    inst = [(c, hk) for c in range(len(q_chunks)) for hk in range(n_kv)]

    fill = []
    for hk in range(n_kv):
        blocks = [jnp.full((CHUNK, width), sink_ref[hk * 2 * groups_per_kv + j], F32)
                  for j in range(2 * groups_per_kv)]
        fill.append(jnp.where(col == n_keys, jnp.concatenate(blocks, axis=0), NEG_INF))

    def scores(c, hk):
        blocks = []
        for m in range(groups_per_kv):
            lo = (hk * groups_per_kv + m) * LANES
            qg = q_chunks[c][:, lo:lo + LANES] * ATTN_SCALE
            zero = jnp.zeros_like(qg)
            blocks += [jnp.where(in_a, qg, zero), jnp.where(in_a, zero, qg)]
        s = _dot_nt(jnp.concatenate(blocks, axis=0), keys[c][hk])
        keep = real_key if valids[c] is None else jnp.logical_and(real_key, valids[c])
        return jnp.where(keep, s, fill[hk])

    s = [scores(c, hk) for c, hk in inst]
    p = [jnp.exp(x - jnp.max(x, axis=-1, keepdims=True)).astype(BF16) for x in s]
    v_ext = [jnp.concatenate([jnp.where(real_row, vals[c][hk], 0.0).astype(BF16), ones], axis=1)
             for c, hk in inst]
    pv = [jnp.dot(x, v, preferred_element_type=F32) for x, v in zip(p, v_ext)]
    o = [x[:, :LANES] / x[:, LANES:] for x in pv]
    rows = []
    for c in range(len(q_chunks)):
        groups = []
        for hk in range(n_kv):
            oc = o[c * n_kv + hk]
            for m in range(groups_per_kv):
                groups.append(jnp.where(in_a, oc[2 * m * CHUNK:(2 * m + 1) * CHUNK],
                                        oc[(2 * m + 1) * CHUNK:(2 * m + 2) * CHUNK]))
        rows.append(jnp.concatenate(groups, axis=1))
    return jnp.concatenate(rows, axis=0)


def _attn_ffn_kernel(sink_ref, q_ref, kc_ref, vc_ref, kp_ref, vp_ref, x_ref, wo_ref, bo_ref, gain_ref,
                     win_ref, wout_ref, gfin_ref, o_ref, *, streaming, ff_chunk):
    tm = x_ref.shape[-2]
    n_chunks = tm // CHUNK
    kvw = kc_ref.shape[-1]
    n_keys = WINDOW + CHUNK

    def dup(arr):
        in_a = lax.broadcasted_iota(jnp.int32, (arr.shape[0], LANES), 1) < HEAD
        res = []
        for j in range(kvw // LANES):
            res += list(_dup_heads(arr[:, j * LANES:(j + 1) * LANES], in_a))
        return res

    filler = jnp.zeros((CHUNK, kvw), F32)
    if streaming:
        q = q_ref[0]
        k_all = dup(jnp.concatenate([kp_ref[0], kc_ref[0], filler], axis=0))
        v_all = dup(jnp.concatenate([vp_ref[0], vc_ref[0], filler], axis=0))
        first_tile = pl.program_id(1) == 0
        key_idx = lax.broadcasted_iota(jnp.int32, (1, n_keys + CHUNK), 1)
    else:
        q = q_ref[...]

    q_chunks, keys, vals, valids = [], [], [], []
    for c in range(n_chunks):
        q_chunks.append(q[c * CHUNK:(c + 1) * CHUNK])
        valid = None
        if streaming:
            keys.append([ka[c * CHUNK:c * CHUNK + n_keys + CHUNK] for ka in k_all])
            vals.append([va[c * CHUNK:c * CHUNK + n_keys + CHUNK] for va in v_all])
            if c * CHUNK < WINDOW:
                valid = jnp.logical_or(key_idx + c * CHUNK >= WINDOW, jnp.logical_not(first_tile))
        else:
            keys.append(dup(jnp.concatenate([kp_ref[c], kc_ref[c * CHUNK:(c + 1) * CHUNK], filler], axis=0)))
            vals.append(dup(jnp.concatenate([vp_ref[c], vc_ref[c * CHUNK:(c + 1) * CHUNK], filler], axis=0)))
        valids.append(valid)
    o = _attend(q_chunks, keys, vals, valids, sink_ref)

    x = x_ref[0] if streaming else x_ref[...]
    x1 = x + _dot(o, wo_ref[...]) + bo_ref[...]
    x2 = _swiglu_residual(x1, gain_ref[...], win_ref, wout_ref, ff_chunk)
    y = _rms(x2) * gfin_ref[...]
    if streaming:
        o_ref[0] = y
    else:
        o_ref[...] = y


def _attn_ffn(q, k_new, v_new, k_prev, v_prev, x, sinks, w_o, b_o, gain, w_in, w_out, gain_fin, tm, ff_chunk,
              layer):
    d = x.shape[-1]
    kvw = k_new.shape[-1]
    streaming = k_prev is None
    smem = pl.BlockSpec(memory_space=pltpu.SMEM)
    weights = (w_o, b_o.reshape(1, d), gain.reshape(1, d), w_in, w_out, gain_fin.reshape(1, d))
    if streaming:
        bsz, t_len, _ = x.shape
        assert tm % WINDOW == 0
        per_win = tm // WINDOW
        grid = (bsz, t_len // tm)
        tile = lambda w: pl.BlockSpec((1, tm, w), lambda b, t: (b, t, 0))
        prev = pl.BlockSpec((1, WINDOW, kvw), lambda b, t: (b, jnp.maximum(t * per_win - 1, 0), 0))
        in_specs = ([smem, tile(d), tile(kvw), tile(kvw), prev, prev, tile(d)]
                    + [_weight_spec(a, layer) for a in weights])
        args = (sinks, q, k_new, v_new, k_new, v_new, x) + weights
        out_spec = tile(d)
        sem = ("arbitrary", "arbitrary")
    else:
        n = x.shape[0]
        grid = (1,)
        full = lambda arr: pl.BlockSpec(arr.shape, lambda i: (0,) * arr.ndim)
        args = (sinks, q, k_new, v_new, k_prev, v_prev, x) + weights
        in_specs = [smem] + [full(a) for a in args[1:7]] + [_weight_spec(a, layer) for a in weights]
        out_spec = pl.BlockSpec((n, d), lambda i: (0, 0))
        sem = ("arbitrary",)
    return pl.pallas_call(
        functools.partial(_attn_ffn_kernel, streaming=streaming, ff_chunk=ff_chunk),
        grid=grid,
        in_specs=in_specs,
        out_specs=out_spec,
        out_shape=jax.ShapeDtypeStruct(x.shape, F32),
        compiler_params=pltpu.CompilerParams(dimension_semantics=sem, vmem_limit_bytes=VMEM_LIMIT_BYTES),
    )(*args)


def _tile(n, cap):
    t = min(n, cap)
    while n % t:
        t -= 1
    return t


def kernel(x_prompt, x_sample, state_wkv, state_shift, cache_k, cache_v, norm_mix, norm_ffn, rw_mu, rw_w_rkv, rw_w0, rw_w1, rw_w2, rw_a0, rw_a1, rw_a2, rw_g1, rw_g2, rw_k_k, rw_k_a, rw_r_k, rw_lnx_w, rw_lnx_b, rw_w_o, kv_norm, w_kv, b_kv, w_q, b_q, attn_sinks, w_o, b_o, ffn_w_in, ffn_w_out, norm_final):
    depth, d = norm_mix.shape
    assert depth == 2 and rw_mu.shape[0] == 1 and w_q.shape[0] == 1, "one RWKV layer followed by one attention layer"
    n_ff = ffn_w_out.shape[1]
    ff_chunk = FF_SLICE if n_ff % FF_SLICE == 0 else n_ff
    bf = lambda w: w.astype(BF16)
    wrkv, w1, w2, a1, a2, g1, g2 = map(bf, (rw_w_rkv[0], rw_w1[0], rw_w2[0], rw_a1[0], rw_a2[0], rw_g1[0], rw_g2[0]))
    rw_wo, wq, wkv, wo = bf(rw_w_o[0]), bf(w_q[0]), bf(w_kv), bf(w_o[0])
    win, wout = bf(ffn_w_in), bf(ffn_w_out)
    sinks = attn_sinks[0].reshape(-1)

    def run(x, pos_base, shift0, st0, win_k, win_v):
        bsz, t_len, _ = x.shape
        n_rows = bsz * t_len
        n_sub = _tile(t_len // CHUNK, RWKV_CHUNKS_PER_STEP)
        z, g, shift, st = _rwkv_layer(
            x, shift0, st0, norm_mix[0], rw_mu[0], wrkv, rw_w0[0], w1, w2, rw_a0[0], a1, a2, g1, g2,
            rw_k_k[0], rw_k_a[0], rw_r_k[0].reshape(-1), rw_lnx_w[0], rw_lnx_b[0],
            n_seq=_tile(bsz, RWKV_CHUNKS_PER_STEP // n_sub), n_sub=n_sub)
        flat = lambda t: t.reshape(n_rows, t.shape[-1])
        x1, q, k_new, v_new, *tails = _ffn_qkv(
            flat(z), flat(g), flat(x), rw_wo, norm_ffn[0], win, wout, norm_mix[1], kv_norm, wq, b_q[0],
            wkv, b_kv, t_len, pos_base, tm=_tile(n_rows, ROW_TILE), ff_chunk=ff_chunk, layer=0)
        kvw = k_new.shape[-1]
        k_new = k_new.reshape(bsz, t_len, kvw)
        v_new = v_new.reshape(bsz, t_len, kvw)
        if win_k is None:
            y = _attn_ffn(q.reshape(bsz, t_len, d), k_new, v_new, None, None, x1.reshape(bsz, t_len, d), sinks,
                          wo, b_o[0], norm_ffn[1], win, wout, norm_final, tm=_tile(t_len, ROW_TILE),
                          ff_chunk=ff_chunk, layer=1)
            k_state, v_state = tails if tails else (k_new[:, -WINDOW:], v_new[:, -WINDOW:])
        else:
            assert t_len == CHUNK
            kp = win_k.reshape(bsz, WINDOW, kvw)
            vp = win_v.reshape(bsz, WINDOW, kvw)
            y = _attn_ffn(q, k_new.reshape(n_rows, kvw), v_new.reshape(n_rows, kvw), kp, vp, x1, sinks,
                          wo, b_o[0], norm_ffn[1], win, wout, norm_final, tm=n_rows, ff_chunk=ff_chunk,
                          layer=1)
            k_state = jnp.concatenate([kp, k_new], axis=1)[:, -WINDOW:]
            v_state = jnp.concatenate([vp, v_new], axis=1)[:, -WINDOW:]
        n_kv = kvw // HEAD
        st = st[None]
        return (y.reshape(bsz, t_len, d), st, shift.reshape(1, bsz, d),
                k_state.reshape(bsz, WINDOW, n_kv, HEAD), v_state.reshape(bsz, WINDOW, n_kv, HEAD))

    bp = x_prompt.shape[0]
    zeros_state = jnp.zeros((bp, HEAD, d), F32)
    y_p, wkv_p, shift_p, k_p, v_p = run(x_prompt, 0, jnp.zeros((bp, d), F32), zeros_state, None, None)
    st_s = state_wkv[0].transpose(0, 3, 1, 2).reshape(x_sample.shape[0], HEAD, d)
    y_s, wkv_s, shift_s, k_s, v_s = run(x_sample, PAST_LEN, state_shift[0], st_s, cache_k, cache_v)
    return (y_p, y_s, wkv_p, shift_p, k_p, v_p, wkv_s, shift_s, k_s, v_s)
```
